```python
import math, functools
import jax, jax.numpy as jnp
from jax import lax
import numpy as np

D_MODEL = 2048
BATCH = 8
SEQ = 2048
DEPTH = 1
DEC_BATCH = 128
DEC_SEQ = 4
PAST_LEN = 2048
PAGE_SIZE = 128

MIX_W = D_MODEL
HG_W = MIX_W // 2
SB_W = MIX_W - HG_W
HG_DK = 128
HG_HEADS = HG_W // HG_DK
HG_DV = HG_W // HG_HEADS
SB_DH = 128
SB_HEADS = SB_W // SB_DH
IN_W = 4 * HG_W + 3 * SB_W
HG_CHUNK = 64
SB_QBLOCK = 128
SB_SCALE = 1.0 / math.sqrt(SB_DH)
SB_BIAS_INIT = -6.0
N_GROUPS = 4
EXPERTS_PER_GROUP = 8
N_EXPERTS = N_GROUPS * EXPERTS_PER_GROUP
TOP_K = 2
EXPERT_FF = D_MODEL // 4
ALPHA = (2.0 * DEPTH) ** 0.25
BETA = (8.0 * DEPTH) ** -0.25
LN_EPS = 1e-5
RMS_EPS = 1e-6

kernel_name = 'hymba_hgrn2_stickbreak_hmoe_step'


def layer_norm(x, g, b):
    xf = x.astype(jnp.float32)
    mu = jnp.mean(xf, axis=-1, keepdims=True)
    var = jnp.mean(jnp.square(xf - mu), axis=-1, keepdims=True)
    return ((xf - mu) * lax.rsqrt(var + LN_EPS) * g + b).astype(x.dtype)


def rms_norm(x, g):
    xf = x.astype(jnp.float32)
    return xf * lax.rsqrt(jnp.mean(jnp.square(xf), axis=-1, keepdims=True) + RMS_EPS) * g


def hgrn2_chunked(q, log_f, k, v, s0, chunk):
    B, T, H, _ = q.shape
    n = T // chunk

    def to_chunks(a):
        return a.reshape(B, n, chunk, H, a.shape[-1]).transpose(1, 0, 3, 2, 4)

    tri = jnp.tril(jnp.ones((chunk, chunk), bool))

    def step(S, inp):
        qc, gc, kc, vc = inp
        b = jnp.cumsum(gc, axis=2)
        o_inter = jnp.einsum('bhtk,bhkv->bhtv', qc * jnp.exp(b), S)
        diff = b[:, :, :, None, :] - b[:, :, None, :, :]
        decay = jnp.exp(jnp.where(tri[:, :, None], diff, -jnp.inf))
        a = jnp.einsum('bhtk,bhsk,bhtsk->bhts', qc, kc, decay)
        o_intra = jnp.einsum('bhts,bhsv->bhtv', a, vc)
        b_last = b[:, :, -1]
        S_new = jnp.exp(b_last)[..., None] * S + jnp.einsum(
            'bhsk,bhsv->bhkv', kc * jnp.exp(b_last[:, :, None] - b), vc)
        return S_new, o_inter + o_intra

    s_fin, o = lax.scan(step, s0, (to_chunks(q), to_chunks(log_f), to_chunks(k), to_chunks(v)))
    o = o.transpose(1, 0, 3, 2, 4).reshape(B, T, H, v.shape[-1])
    return o, s_fin


def stick_breaking(q, k, v, bias, q_pos, k_pos):
    z = (jnp.einsum('bqhd,bkhd->bhqk', q, k) * SB_SCALE
         + bias.astype(jnp.float32)[None, :, None, None])
    mask = k_pos[None, :] < q_pos[:, None]
    log_1mb = jnp.where(mask, jax.nn.log_sigmoid(-z), 0.0)
    later = lax.cumsum(log_1mb, axis=3, reverse=True) - log_1mb
    a = jnp.where(mask, jnp.exp(jax.nn.log_sigmoid(z) + later), 0.0)
    return jnp.einsum('bhqk,bkhd->bqhd', a, v)


def sb_prompt(q, k, v, bias):
    B, T, H, Dh = q.shape
    nb = T // SB_QBLOCK
    qb = q.reshape(B, nb, SB_QBLOCK, H, Dh).transpose(1, 0, 2, 3, 4)
    k_pos = jnp.arange(T)

    def one(args):
        q_blk, start = args
        return stick_breaking(q_blk, k, v, bias, start + jnp.arange(SB_QBLOCK), k_pos)

    o = lax.map(one, (qb, jnp.arange(nb) * SB_QBLOCK))
    return o.transpose(1, 0, 2, 3, 4).reshape(B, T, H, Dh)


def sb_sample(q, k, v, bias, cache_k_l, cache_v_l, page_table):
    DB, S, H, Dh = q.shape
    past = page_table.shape[1] * cache_k_l.shape[1]
    pk = cache_k_l[page_table].reshape(DB, past, H, Dh).astype(jnp.float32)
    pv = cache_v_l[page_table].reshape(DB, past, H, Dh).astype(jnp.float32)
    keys = jnp.concatenate([pk, k], axis=1)
    vals = jnp.concatenate([pv, v], axis=1)
    return stick_breaking(q, keys, vals, bias, past + jnp.arange(S), jnp.arange(past + S))


def hier_moe(u, w_gr, b_gr, w_er, b_er, w_gate, w_up, w_down):
    B, T, D = u.shape
    h = u.reshape(B * T, D)
    g_logits = (h @ w_gr + b_gr).astype(jnp.float32)
    g_idx = jnp.argmax(g_logits, axis=-1)
    g_w = jnp.take_along_axis(jax.nn.softmax(g_logits, axis=-1), g_idx[:, None], axis=-1)
    e_logits = (jnp.einsum('nd,dge->nge', h, w_er) + b_er).astype(jnp.float32)
    e_sel = jnp.take_along_axis(e_logits, g_idx[:, None, None], axis=1)[:, 0]
    top_val, top_idx = lax.top_k(e_sel, TOP_K)
    gate = g_w * jax.nn.softmax(top_val, axis=-1)
    eid = g_idx[:, None] * EXPERTS_PER_GROUP + top_idx
    dense_gate = jnp.sum(jax.nn.one_hot(eid, N_EXPERTS, dtype=jnp.float32) * gate[..., None], axis=1)
    y = jnp.zeros((B * T, D), jnp.float32)
    for e in range(N_EXPERTS):
        hid = jax.nn.silu(h @ w_gate[e]) * (h @ w_up[e])
        y = y + dense_gate[:, e:e + 1] * (hid @ w_down[e])
    return y.reshape(B, T, D).astype(u.dtype)


def decoder_layer(x, c, s0, sb_attend, lb, w_ada, b_ada, w_in, hg_norm, sb_norm, sb_bias, w_o,
                  ln1_g, ln1_b, w_gr, b_gr, w_er, b_er, w_gate, w_up, w_down, ln2_g, ln2_b):
    B, T, _ = x.shape
    ada = jnp.einsum('bd,de->be', jax.nn.silu(c), w_ada) + b_ada
    sh1, sc1, g1, sh2, sc2, g2 = jnp.split(ada[:, None, :], 6, axis=-1)
    u = x * (1 + sc1) + sh1
    proj = (u @ w_in).astype(jnp.float32)
    hq, hf, hi, hg, sq, sk, sv = jnp.split(proj, np.cumsum([HG_W] * 4 + [SB_W] * 2).tolist(), axis=-1)
    f = lb + (1.0 - lb) * jax.nn.sigmoid(hf)
    hshape = (B, T, HG_HEADS, HG_DK)
    chunk = math.gcd(HG_CHUNK, T)
    o_h, s_fin = hgrn2_chunked(jax.nn.silu(hq).reshape(hshape), jnp.log(f).reshape(hshape),
                               (1.0 - f).reshape(hshape), hi.reshape(B, T, HG_HEADS, HG_DV),
                               s0.astype(jnp.float32), chunk)
    o_h = rms_norm(o_h, hg_norm) * jax.nn.silu(hg.reshape(B, T, HG_HEADS, HG_DV))
    sshape = (B, T, SB_HEADS, SB_DH)
    k_sb = sk.reshape(sshape)
    v_sb = sv.reshape(sshape)
    o_s = rms_norm(sb_attend(sq.reshape(sshape), k_sb, v_sb, sb_bias), sb_norm)
    mix = jnp.concatenate([o_h.reshape(B, T, HG_W), o_s.reshape(B, T, SB_W)], axis=-1).astype(u.dtype) @ w_o
    x = layer_norm(ALPHA * x + g1 * mix, ln1_g, ln1_b)
    u2 = x * (1 + sc2) + sh2
    x = layer_norm(ALPHA * x + g2 * hier_moe(u2, w_gr, b_gr, w_er, b_er, w_gate, w_up, w_down), ln2_g, ln2_b)
    return x, s_fin, k_sb, v_sb


def setup_inputs(seed: int = 0) -> dict:
    key = jax.random.key(seed)
    ks = jax.random.split(key, 27)
    f32 = jnp.float32
    n_pages = PAST_LEN // PAGE_SIZE
    n_used = DEC_BATCH * n_pages
    n_phys = n_used + max(1, n_used // 4)
    L = DEPTH

    def nrm(k, shape, s):
        return jax.random.normal(k, shape, f32) * s

    col_scale = jnp.concatenate([jnp.ones(HG_W), jnp.ones(HG_W), jnp.full(HG_W, BETA), jnp.ones(HG_W),
                                 jnp.ones(SB_W), jnp.ones(SB_W), jnp.full(SB_W, BETA)]).astype(f32)
    page_table = jax.random.permutation(ks[5], n_phys)[:n_used].reshape(DEC_BATCH, n_pages).astype(jnp.int32)
    return {
        'x_prompt': nrm(ks[0], (BATCH, SEQ, D_MODEL), 1.0),
        'x_sample': nrm(ks[1], (DEC_BATCH, DEC_SEQ, D_MODEL), 1.0),
        'cache_k': nrm(ks[2], (L, n_phys, PAGE_SIZE, SB_HEADS, SB_DH), 1.0),
        'cache_v': nrm(ks[3], (L, n_phys, PAGE_SIZE, SB_HEADS, SB_DH), 1.0),
        'state_hgrn': nrm(ks[4], (L, DEC_BATCH, HG_HEADS, HG_DK, HG_DV), 0.5),
        'page_table': page_table,
        'c_prompt': nrm(ks[6], (BATCH, D_MODEL), 1.0),
        'c_sample': nrm(ks[7], (DEC_BATCH, D_MODEL), 1.0),
        'hg_lb_logits': nrm(ks[8], (L + 1, HG_W), 0.5),
        'w_ada': nrm(ks[9], (L, D_MODEL, 6 * D_MODEL), 0.5 * D_MODEL ** -0.5),
        'b_ada': nrm(ks[10], (L, 6 * D_MODEL), 0.02),
        'w_in': nrm(ks[11], (L, D_MODEL, IN_W), D_MODEL ** -0.5) * col_scale,
        'hg_norm': 1.0 + nrm(ks[12], (L, HG_DV), 0.05),
        'sb_norm': 1.0 + nrm(ks[13], (L, SB_DH), 0.05),
        'sb_bias': SB_BIAS_INIT + nrm(ks[26], (L, SB_HEADS), 0.3),
        'w_o': nrm(ks[14], (L, MIX_W, D_MODEL), BETA * MIX_W ** -0.5),
        'ln1_g': 1.0 + nrm(ks[15], (L, D_MODEL), 0.05),
        'ln1_b': nrm(ks[16], (L, D_MODEL), 0.01),
        'w_gr': nrm(ks[17], (L, D_MODEL, N_GROUPS), D_MODEL ** -0.5),
        'b_gr': nrm(ks[18], (L, N_GROUPS), 0.01),
        'w_er': nrm(ks[19], (L, D_MODEL, N_GROUPS, EXPERTS_PER_GROUP), D_MODEL ** -0.5),
        'b_er': nrm(ks[20], (L, N_GROUPS, EXPERTS_PER_GROUP), 0.01),
        'w_gate': nrm(ks[21], (L, N_EXPERTS, D_MODEL, EXPERT_FF), D_MODEL ** -0.5),
        'w_up': nrm(ks[22], (L, N_EXPERTS, D_MODEL, EXPERT_FF), D_MODEL ** -0.5),
        'w_down': nrm(ks[23], (L, N_EXPERTS, EXPERT_FF, D_MODEL), BETA * EXPERT_FF ** -0.5),
        'ln2_g': 1.0 + nrm(ks[24], (L, D_MODEL), 0.05),
        'ln2_b': nrm(ks[25], (L, D_MODEL), 0.01),
    }


def reference(x_prompt, x_sample, cache_k, cache_v, state_hgrn, page_table, c_prompt, c_sample,
              hg_lb_logits, w_ada, b_ada, w_in, hg_norm, sb_norm, sb_bias, w_o, ln1_g, ln1_b,
              w_gr, b_gr, w_er, b_er, w_gate, w_up, w_down, ln2_g, ln2_b):
    lower_bounds = jnp.cumsum(jax.nn.softmax(hg_lb_logits.astype(jnp.float32), axis=0), axis=0)
    hp, hs = x_prompt, x_sample
    s0_prompt = jnp.zeros((x_prompt.shape[0], HG_HEADS, HG_DK, HG_DV), jnp.float32)
    kp_l, vp_l, sp_l, ks_l, vs_l, ss_l = [], [], [], [], [], []
    for l in range(DEPTH):
        lw = (lower_bounds[l], w_ada[l], b_ada[l], w_in[l], hg_norm[l], sb_norm[l], sb_bias[l], w_o[l],
              ln1_g[l], ln1_b[l], w_gr[l], b_gr[l], w_er[l], b_er[l], w_gate[l], w_up[l],
              w_down[l], ln2_g[l], ln2_b[l])
        hp, sp, kp, vp = decoder_layer(hp, c_prompt, s0_prompt, sb_prompt, *lw)
        attend = functools.partial(sb_sample, cache_k_l=cache_k[l], cache_v_l=cache_v[l],
                                   page_table=page_table)
        hs, ss, ksm, vsm = decoder_layer(hs, c_sample, state_hgrn[l], attend, *lw)
        kp_l.append(kp)
        vp_l.append(vp)
        sp_l.append(sp)
        ks_l.append(ksm)
        vs_l.append(vsm)
        ss_l.append(ss)
    return (hp, hs, jnp.stack(kp_l), jnp.stack(vp_l), jnp.stack(sp_l),
            jnp.stack(ks_l), jnp.stack(vs_l), jnp.stack(ss_l))
```

```python
import functools
import math

import jax
import jax.numpy as jnp
from jax import lax
from jax.experimental import pallas as pl
from jax.experimental.pallas import tpu as pltpu

F32 = jnp.float32
BF16 = jnp.bfloat16
HIGHEST = lax.Precision.HIGHEST

LANE = 128
SUBLANE = 8
HEAD_DIM = 128
N_GROUPS = 4
EXPERTS_PER_GROUP = 8
N_EXPERTS = N_GROUPS * EXPERTS_PER_GROUP
LN_EPS = 1e-5
RMS_EPS = 1e-6
VMEM_LIMIT = 56 * 1024 * 1024
HG_CHUNK = 64
HG_SUB = SUBLANE
MOE_TM = 256
ROW_TM = 256


def _cparams(sem):
    return pltpu.CompilerParams(dimension_semantics=sem, vmem_limit_bytes=VMEM_LIMIT)


def _silu(x):
    return x * jax.nn.sigmoid(x)


def _dot(a, b):
    return jnp.dot(a, b, preferred_element_type=F32)


def _dot_nt(a, b):
    return lax.dot_general(a, b, (((1,), (1,)), ((), ())), preferred_element_type=F32)


def _dot_f32(a, b):
    return jnp.dot(a, b, preferred_element_type=F32, precision=HIGHEST)


def _dot_nt_f32(a, b):
    return lax.dot_general(a, b, (((1,), (1,)), ((), ())), preferred_element_type=F32, precision=HIGHEST)


def _ada_kernel(c_ref, w_ref, b_ref, o_ref):
    o_ref[...] = _dot_f32(_silu(c_ref[...]), w_ref[...]) + b_ref[...]


def _ada(c, w_ada, b_ada):
    rows, d = c.shape
    n = w_ada.shape[1]
    tn = 1024
    return pl.pallas_call(
        _ada_kernel,
        out_shape=jax.ShapeDtypeStruct((rows, n), F32),
        grid=(n // tn,),
        in_specs=[pl.BlockSpec((rows, d), lambda j: (0, 0)),
                  pl.BlockSpec((d, tn), lambda j: (0, j)),
                  pl.BlockSpec((1, tn), lambda j: (0, j))],
        out_specs=pl.BlockSpec((rows, tn), lambda j: (0, j)),
        compiler_params=_cparams(("arbitrary",)),
        name="ada",
    )(c, w_ada, b_ada.reshape(1, n))


def _in_proj_kernel(x_ref, sc_ref, sh_ref, w_ref, o_ref, u_ref):
    @pl.when(pl.program_id(2) == 0)
    def _():
        u_ref[...] = (x_ref[0] * (1.0 + sc_ref[0]) + sh_ref[0]).astype(BF16)

    o_ref[0] = _dot(u_ref[...], w_ref[...])


def _in_proj(x, sc, sh, w_bf16, tm):
    b, t, d = x.shape
    n = w_bf16.shape[1]
    tn = 1024
    per_row = sc.shape[1] != 1
    cond_block = (1, tm, d) if per_row else (1, 1, d)
    cond_map = (lambda bi, ti, ni: (bi, ti, 0)) if per_row else (lambda bi, ti, ni: (bi, 0, 0))
    return pl.pallas_call(
        _in_proj_kernel,
        out_shape=jax.ShapeDtypeStruct((b, t, n), F32),
        grid=(b, t // tm, n // tn),
        in_specs=[pl.BlockSpec((1, tm, d), lambda bi, ti, ni: (bi, ti, 0)),
                  pl.BlockSpec(cond_block, cond_map),
                  pl.BlockSpec(cond_block, cond_map),
                  pl.BlockSpec((d, tn), lambda bi, ti, ni: (0, ni))],
        out_specs=pl.BlockSpec((1, tm, tn), lambda bi, ti, ni: (bi, ti, ni)),
        scratch_shapes=[pltpu.VMEM((tm, d), BF16)],
        compiler_params=_cparams(("arbitrary", "arbitrary", "arbitrary")),
        name="in_proj",
    )(x, sc, sh, w_bf16)


def _hgrn_levels(chunk):
    levels = []
    m = HG_SUB
    while m < chunk:
        levels.append(m)
        m *= 2
    return levels


def _hgrn_static(chunk):
    t = jnp.arange(chunk)
    mats = [(t[None, :] <= t[:, None])]
    for m in _hgrn_levels(chunk):
        boundary = (t // (2 * m)) * (2 * m) + m - 1
        mats.append(t[None, :] <= boundary[:, None])
    return jnp.concatenate(mats, axis=0).astype(F32)


def _hgrn_kernel(p_ref, s0_ref, lb_ref, gn_ref, cum_ref, o_ref, sfin_ref, s_ref, *, chunk, heads, t_valid):
    ci = pl.program_id(1)

    @pl.when(ci == 0)
    def _():
        s_ref[...] = s0_ref[0]

    hw = heads * HEAD_DIM
    row = lax.broadcasted_iota(jnp.int32, (chunk, 1), 0)
    valid = (ci * chunk + row) < t_valid
    rr = lax.broadcasted_iota(jnp.int32, (chunk, chunk), 0)
    cc = lax.broadcasted_iota(jnp.int32, (chunk, chunk), 1)
    nsub = chunk // HG_SUB
    sub_row = lax.broadcasted_iota(jnp.int32, (nsub, HG_SUB, 1), 1)
    levels = _hgrn_levels(chunk)
    cum = cum_ref[...]

    for h in range(heads):
        lo = h * HEAD_DIM
        hq = p_ref[0, :, lo:lo + HEAD_DIM]
        hf = p_ref[0, :, hw + lo:hw + lo + HEAD_DIM]
        v = p_ref[0, :, 2 * hw + lo:2 * hw + lo + HEAD_DIM]
        hg = p_ref[0, :, 3 * hw + lo:3 * hw + lo + HEAD_DIM]
        lb = lb_ref[:, lo:lo + HEAD_DIM]
        f = lb + (1.0 - lb) * jax.nn.sigmoid(hf)
        g = jnp.where(valid, jnp.log(f), 0.0)
        k = jnp.where(valid, 1.0 - f, 0.0)
        q = _silu(hq)

        sums = _dot_f32(cum, g)
        b = sums[0:chunk]
        b_last = b[chunk - 1:chunk, :]
        s_old = s_ref[h]

        o = _dot((q * jnp.exp(b)).astype(BF16), s_old.astype(BF16))

        a = jnp.zeros((chunk, chunk), F32)
        for li, m in enumerate(levels):
            bm = sums[(li + 1) * chunk:(li + 2) * chunk]
            upper = (row % (2 * m)) >= m
            qs = jnp.where(upper, q * jnp.exp(jnp.where(upper, b - bm, 0.0)), 0.0)
            ks = jnp.where(upper, 0.0, k * jnp.exp(jnp.where(upper, 0.0, bm - b)))
            same = (rr // (2 * m)) == (cc // (2 * m))
            a = a + jnp.where(same, _dot_nt_f32(qs, ks), 0.0)

        q3 = q.reshape(nsub, HG_SUB, HEAD_DIM)
        k3 = k.reshape(nsub, HG_SUB, HEAD_DIM)
        b3 = b.reshape(nsub, HG_SUB, HEAD_DIM)
        for j in range(HG_SUB):
            kj = k3[:, j:j + 1, :]
            bj = b3[:, j:j + 1, :]
            e = jnp.exp(jnp.where(sub_row >= j, b3 - bj, 0.0))
            col = jnp.sum(q3 * kj * e, axis=-1, keepdims=True).reshape(chunk, 1)
            hit = (cc == (rr // HG_SUB) * HG_SUB + j) & ((rr % HG_SUB) >= j)
            a = a + jnp.where(hit, col, 0.0)

        o = o + _dot(a.astype(BF16), v.astype(BF16))

        kd = k * jnp.exp(b_last - b)
        if chunk < HEAD_DIM:
            pad = jnp.zeros((HEAD_DIM - chunk, HEAD_DIM), F32)
            kd = jnp.concatenate([kd, pad], axis=0)
            vp = jnp.concatenate([v, pad], axis=0)
        else:
            vp = v
        decay_col = jnp.broadcast_to(jnp.exp(b_last), (HEAD_DIM, HEAD_DIM)).T
        s_ref[h] = decay_col * s_old + _dot(kd.T.astype(BF16), vp.astype(BF16))

        ms = jnp.mean(o * o, axis=-1, keepdims=True)
        on = o * lax.rsqrt(ms + RMS_EPS) * gn_ref[...]
        o_ref[0, :, lo:lo + HEAD_DIM] = (on * _silu(hg)).astype(o_ref.dtype)

    @pl.when(ci == pl.num_programs(1) - 1)
    def _():
        sfin_ref[0] = s_ref[...]


def _hgrn(proj, s0, lb, hg_norm, chunk, t_valid):
    b, t, _ = proj.shape
    heads = s0.shape[1]
    hw = heads * HEAD_DIM
    cum = _hgrn_static(chunk)
    kern = functools.partial(_hgrn_kernel, chunk=chunk, heads=heads, t_valid=t_valid)
    return pl.pallas_call(
        kern,
        out_shape=(jax.ShapeDtypeStruct((b, t, hw), BF16),
                   jax.ShapeDtypeStruct(s0.shape, F32)),
        grid=(b, t // chunk),
        in_specs=[pl.BlockSpec((1, chunk, 4 * hw), lambda bi, ci: (bi, ci, 0)),
                  pl.BlockSpec((1, heads, HEAD_DIM, HEAD_DIM), lambda bi, ci: (bi, 0, 0, 0)),
                  pl.BlockSpec((1, hw), lambda bi, ci: (0, 0)),
                  pl.BlockSpec((1, HEAD_DIM), lambda bi, ci: (0, 0)),
                  pl.BlockSpec(cum.shape, lambda bi, ci: (0, 0))],
        out_specs=(pl.BlockSpec((1, chunk, hw), lambda bi, ci: (bi, ci, 0)),
                   pl.BlockSpec((1, heads, HEAD_DIM, HEAD_DIM), lambda bi, ci: (bi, 0, 0, 0))),
        scratch_shapes=[pltpu.VMEM((heads, HEAD_DIM, HEAD_DIM), F32)],
        compiler_params=_cparams(("arbitrary", "arbitrary")),
        name="hgrn",
    )(proj, s0, lb.reshape(1, hw), hg_norm.reshape(1, HEAD_DIM), cum)


def _sb_block(q_bf, k, v, bias, mask, carry, acc, upper_bf):
    z = _dot_nt(q_bf, k.astype(BF16)) + bias
    sp = jnp.maximum(z, 0.0) + jnp.log1p(jnp.exp(-jnp.abs(z)))
    lp = -sp if mask is None else jnp.where(mask, -sp, 0.0)
    lp_hi = lp.astype(BF16)
    lp_lo = (lp - lp_hi.astype(F32)).astype(BF16)
    cs = _dot(lp_hi, upper_bf) + _dot(lp_lo, upper_bf)
    logw = (z - sp) + (carry + cs - lp)
    w = jnp.exp(logw)
    if mask is not None:
        w = jnp.where(mask, w, 0.0)
    acc = acc + _dot(w.astype(BF16), v.astype(BF16))
    carry = carry + cs[:, 0:1]
    return carry, acc


def _upper_ones(n):
    r = lax.broadcasted_iota(jnp.int32, (n, n), 0)
    c = lax.broadcasted_iota(jnp.int32, (n, n), 1)
    return (r >= c).astype(BF16)


def _sb_prompt_kernel(bias_ref, q_ref, k_ref, v_ref, gn_ref, o_ref, *, qb, scale):
    h = pl.program_id(1)
    i = pl.program_id(2)
    bias = bias_ref[h]
    q_bf = (q_ref[0] * scale).astype(BF16)
    upper = _upper_ones(qb)
    q_pos = i * qb + lax.broadcasted_iota(jnp.int32, (qb, qb), 0)
    col = lax.broadcasted_iota(jnp.int32, (qb, qb), 1)

    def body(step, state):
        carry, acc = state
        j = i - step
        start = pl.multiple_of(j * qb, qb)
        k = k_ref[0, pl.ds(start, qb), :]
        v = v_ref[0, pl.ds(start, qb), :]
        mask = (start + col) < q_pos
        return _sb_block(q_bf, k, v, bias, mask, carry, acc, upper)

    carry0 = jnp.zeros((qb, 1), F32)
    acc0 = jnp.zeros((qb, HEAD_DIM), F32)
    _, acc = lax.fori_loop(0, i + 1, body, (carry0, acc0))
    ms = jnp.mean(acc * acc, axis=-1, keepdims=True)
    o_ref[0] = (acc * lax.rsqrt(ms + RMS_EPS) * gn_ref[...]).astype(o_ref.dtype)


def _sb_prompt(proj, sb_bias, sb_norm, heads, col0):
    b, t, _ = proj.shape
    qb = 128
    kern = functools.partial(_sb_prompt_kernel, qb=qb, scale=1.0 / math.sqrt(HEAD_DIM))
    grid_spec = pltpu.PrefetchScalarGridSpec(
        num_scalar_prefetch=0,
        grid=(b, heads, t // qb),
        in_specs=[pl.BlockSpec(memory_space=pltpu.SMEM),
                  pl.BlockSpec((1, qb, HEAD_DIM), lambda bi, hi, qi: (bi, qi, col0 + hi)),
                  pl.BlockSpec((1, t, HEAD_DIM), lambda bi, hi, qi: (bi, 0, col0 + heads + hi)),
                  pl.BlockSpec((1, t, HEAD_DIM), lambda bi, hi, qi: (bi, 0, col0 + 2 * heads + hi)),
                  pl.BlockSpec((1, HEAD_DIM), lambda bi, hi, qi: (0, 0))],
        out_specs=pl.BlockSpec((1, qb, HEAD_DIM), lambda bi, hi, qi: (bi, qi, hi)),
    )
    return pl.pallas_call(
        kern,
        out_shape=jax.ShapeDtypeStruct((b, t, heads * HEAD_DIM), BF16),
        grid_spec=grid_spec,
        compiler_params=_cparams(("arbitrary", "arbitrary", "arbitrary")),
        name="sb_prompt",
    )(sb_bias, proj, proj, proj, sb_norm.reshape(1, HEAD_DIM))


def _sb_sample_kernel(pt_ref, bias_ref, q_ref, ko_ref, vo_ref, kc_ref, vc_ref, gn_ref, o_ref,
                      carry_ref, acc_ref, *, heads, s_valid, scale):
    p = pl.program_id(1)
    rows = q_ref.shape[1]
    page = kc_ref.shape[1]
    upper = _upper_ones(page)

    @pl.when(p == 0)
    def _():
        r = lax.broadcasted_iota(jnp.int32, (rows, page), 0)
        c = lax.broadcasted_iota(jnp.int32, (rows, page), 1)
        mask = (c < r) & (c < s_valid)
        pad = jnp.zeros((page - rows, HEAD_DIM), F32)
        for h in range(heads):
            lo = h * HEAD_DIM
            q_bf = (q_ref[0, :, lo:lo + HEAD_DIM] * scale).astype(BF16)
            k = jnp.concatenate([ko_ref[0, :, lo:lo + HEAD_DIM], pad], axis=0)
            v = jnp.concatenate([vo_ref[0, :, lo:lo + HEAD_DIM], pad], axis=0)
            carry, acc = _sb_block(q_bf, k, v, bias_ref[h], mask, jnp.zeros((rows, 1), F32),
                                   jnp.zeros((rows, HEAD_DIM), F32), upper)
            carry_ref[h] = jnp.broadcast_to(carry, (rows, HEAD_DIM))
            acc_ref[h] = acc

    for h in range(heads):
        lo = h * HEAD_DIM
        q_bf = (q_ref[0, :, lo:lo + HEAD_DIM] * scale).astype(BF16)
        k = kc_ref[0, :, lo:lo + HEAD_DIM]
        v = vc_ref[0, :, lo:lo + HEAD_DIM]
        carry, acc = _sb_block(q_bf, k, v, bias_ref[h], None, carry_ref[h][:, 0:1], acc_ref[h], upper)
        carry_ref[h] = jnp.broadcast_to(carry, (rows, HEAD_DIM))
        acc_ref[h] = acc

    @pl.when(p == pl.num_programs(1) - 1)
    def _():
        for h in range(heads):
            lo = h * HEAD_DIM
            acc = acc_ref[h]
            ms = jnp.mean(acc * acc, axis=-1, keepdims=True)
            o_ref[0, :, lo:lo + HEAD_DIM] = (acc * lax.rsqrt(ms + RMS_EPS) * gn_ref[...]).astype(o_ref.dtype)


def _sb_sample(q, k_own, v_own, cache_k, cache_v, page_table, sb_bias, sb_norm, heads, s_valid):
    db, rows, hw = q.shape
    n_pages = page_table.shape[1]
    page = cache_k.shape[1]
    kern = functools.partial(_sb_sample_kernel, heads=heads, s_valid=s_valid, scale=1.0 / math.sqrt(HEAD_DIM))

    def page_map(bi, pi, pt):
        return (pt[bi * n_pages + (n_pages - 1 - pi)], 0, 0)

    own_map = lambda bi, pi, pt: (bi, 0, 0)
    grid_spec = pltpu.PrefetchScalarGridSpec(
        num_scalar_prefetch=1,
        grid=(db, n_pages),
        in_specs=[pl.BlockSpec(memory_space=pltpu.SMEM),
                  pl.BlockSpec((1, rows, hw), own_map),
                  pl.BlockSpec((1, rows, hw), own_map),
                  pl.BlockSpec((1, rows, hw), own_map),
                  pl.BlockSpec((1, page, hw), page_map),
                  pl.BlockSpec((1, page, hw), page_map),
                  pl.BlockSpec((1, HEAD_DIM), lambda bi, pi, pt: (0, 0))],
        out_specs=pl.BlockSpec((1, rows, hw), own_map),
        scratch_shapes=[pltpu.VMEM((heads, rows, HEAD_DIM), F32),
                        pltpu.VMEM((heads, rows, HEAD_DIM), F32)],
    )
    return pl.pallas_call(
        kern,
        out_shape=jax.ShapeDtypeStruct((db, rows, hw), BF16),
        grid_spec=grid_spec,
        compiler_params=_cparams(("arbitrary", "arbitrary")),
        name="sb_sample",
    )(page_table.reshape(-1), sb_bias, q, k_own, v_own, cache_k, cache_v, sb_norm.reshape(1, HEAD_DIM))


def _layer_norm(y, g, b):
    mu = jnp.mean(y, axis=-1, keepdims=True)
    yc = y - mu
    var = jnp.mean(yc * yc, axis=-1, keepdims=True)
    return yc * lax.rsqrt(var + LN_EPS) * g + b


def _out_proj_kernel(oh_ref, os_ref, x_ref, g1_ref, sc2_ref, sh2_ref, wt_ref, wb_ref, lg_ref, lbias_ref,
                     wr_ref, br_ref, x1_ref, u2_ref, route_ref, *, alpha):
    mix = _dot(oh_ref[0], wt_ref[...]) + _dot(os_ref[0], wb_ref[...])
    x1 = _layer_norm(alpha * x_ref[0] + g1_ref[0] * mix, lg_ref[...], lbias_ref[...])
    x1_ref[0] = x1
    u2 = x1 * (1.0 + sc2_ref[0]) + sh2_ref[0]
    u2_ref[0] = u2

    logits = _dot_f32(u2, wr_ref[...]) + br_ref[...]
    lane = lax.broadcasted_iota(jnp.int32, logits.shape, 1)
    neg = -jnp.inf
    gl = jnp.where(lane < N_GROUPS, logits, neg)
    gmax = jnp.max(gl, axis=-1, keepdims=True)
    gidx = jnp.min(jnp.where(gl == gmax, lane, LANE), axis=-1, keepdims=True)
    g_w = 1.0 / jnp.sum(jnp.exp(gl - gmax), axis=-1, keepdims=True)
    in_group = (lane >= N_GROUPS) & (lane < N_GROUPS + N_EXPERTS) & \
               (((lane - N_GROUPS) // EXPERTS_PER_GROUP) == gidx)
    el = jnp.where(in_group, logits, neg)
    v1 = jnp.max(el, axis=-1, keepdims=True)
    i1 = jnp.min(jnp.where(el == v1, lane, LANE), axis=-1, keepdims=True)
    el2 = jnp.where(lane == i1, neg, el)
    v2 = jnp.max(el2, axis=-1, keepdims=True)
    i2 = jnp.min(jnp.where(el2 == v2, lane, LANE), axis=-1, keepdims=True)
    e21 = jnp.exp(v2 - v1)
    p1 = 1.0 / (1.0 + e21)
    p2 = e21 * p1
    route = jnp.where(lane == 0, (i1 - N_GROUPS).astype(F32),
            jnp.where(lane == 1, (i2 - N_GROUPS).astype(F32),
            jnp.where(lane == 2, g_w * p1,
            jnp.where(lane == 3, g_w * p2, 0.0))))
    route_ref[0] = route


def _out_proj(o_h, o_s, x, g1, sc2, sh2, w_o_bf16, ln_g, ln_b, w_r, b_r, alpha, tm):
    b, t, d = x.shape
    hw = o_h.shape[2]
    per_row = g1.shape[1] != 1
    cond_block = (1, tm, d) if per_row else (1, 1, d)
    cond_map = (lambda bi, ti: (bi, ti, 0)) if per_row else (lambda bi, ti: (bi, 0, 0))
    row_map = lambda bi, ti: (bi, ti, 0)
    fixed = lambda bi, ti: (0, 0)
    kern = functools.partial(_out_proj_kernel, alpha=alpha)
    return pl.pallas_call(
        kern,
        out_shape=(jax.ShapeDtypeStruct((b, t, d), F32),
                   jax.ShapeDtypeStruct((b, t, d), F32),
                   jax.ShapeDtypeStruct((b, t, LANE), F32)),
        grid=(b, t // tm),
        in_specs=[pl.BlockSpec((1, tm, hw), row_map),
                  pl.BlockSpec((1, tm, hw), row_map),
                  pl.BlockSpec((1, tm, d), row_map),
                  pl.BlockSpec(cond_block, cond_map),
                  pl.BlockSpec(cond_block, cond_map),
                  pl.BlockSpec(cond_block, cond_map),
                  pl.BlockSpec((hw, d), lambda bi, ti: (0, 0)),
                  pl.BlockSpec((hw, d), lambda bi, ti: (1, 0)),
                  pl.BlockSpec((1, d), fixed),
                  pl.BlockSpec((1, d), fixed),
                  pl.BlockSpec((d, LANE), fixed),
                  pl.BlockSpec((1, LANE), fixed)],
        out_specs=(pl.BlockSpec((1, tm, d), row_map),
                   pl.BlockSpec((1, tm, d), row_map),
                   pl.BlockSpec((1, tm, LANE), row_map)),
        compiler_params=_cparams(("arbitrary", "arbitrary")),
        name="out_proj",
    )(o_h, o_s, x, g1, sc2, sh2, w_o_bf16, w_o_bf16, ln_g.reshape(1, d), ln_b.reshape(1, d), w_r, b_r)


def _row_copy(src_hbm, row, dst, i, sem):
    return pltpu.make_async_copy(src_hbm.at[pl.ds(row, 1)], dst.at[pl.ds(i, 1)], sem)


def _moe_kernel(te_ref, tv_ref, tok_ref, u_hbm, gate_ref, wg_ref, wu_ref, wd_ref, y_ref, x_buf, sem, *, tm):
    t = pl.program_id(0)

    @pl.when(tv_ref[t] != 0)
    def _():
        def start(i, c):
            _row_copy(u_hbm, tok_ref[0, 0, i], x_buf, i, sem.at[0]).start()
            return c

        lax.fori_loop(0, tm, start, 0)

        def wait(i, c):
            _row_copy(u_hbm, 0, x_buf, i, sem.at[0]).wait()
            return c

        lax.fori_loop(0, tm, wait, 0)
        x = x_buf[...].astype(BF16)
        hid = _silu(_dot(x, wg_ref[0])) * _dot(x, wu_ref[0])
        y_ref[...] = _dot(hid.astype(BF16), wd_ref[0]) * gate_ref[...]

    @pl.when(tv_ref[t] == 0)
    def _():
        y_ref[...] = jnp.zeros_like(y_ref)


def _moe(u2, tile_expert, tile_valid, slot_token, slot_gate, wg, wu, wd):
    n, d = u2.shape
    tm = MOE_TM
    n_tiles = slot_token.shape[0]
    ff = wg.shape[2]
    kern = functools.partial(_moe_kernel, tm=tm)
    grid_spec = pltpu.PrefetchScalarGridSpec(
        num_scalar_prefetch=2,
        grid=(n_tiles,),
        in_specs=[pl.BlockSpec((1, 1, tm), lambda t, te, tv: (t, 0, 0), memory_space=pltpu.SMEM),
                  pl.BlockSpec(memory_space=pl.ANY),
                  pl.BlockSpec((tm, 1), lambda t, te, tv: (t, 0)),
                  pl.BlockSpec((1, d, ff), lambda t, te, tv: (te[t], 0, 0)),
                  pl.BlockSpec((1, d, ff), lambda t, te, tv: (te[t], 0, 0)),
                  pl.BlockSpec((1, ff, d), lambda t, te, tv: (te[t], 0, 0))],
        out_specs=pl.BlockSpec((tm, d), lambda t, te, tv: (t, 0)),
        scratch_shapes=[pltpu.VMEM((tm, d), F32), pltpu.SemaphoreType.DMA((1,))],
    )
    return pl.pallas_call(
        kern,
        out_shape=jax.ShapeDtypeStruct((n_tiles * tm, d), F32),
        grid_spec=grid_spec,
        compiler_params=_cparams(("arbitrary",)),
        name="moe",
    )(tile_expert, tile_valid, slot_token, u2, slot_gate, wg, wu, wd)


def _combine_kernel(pos_ref, y_hbm, x1_ref, g2_ref, lg_ref, lbias_ref, o_ref, buf0, buf1, sem, *, tm, alpha):
    def start(i, c):
        _row_copy(y_hbm, pos_ref[0, 0, 2 * i], buf0, i, sem.at[0]).start()
        _row_copy(y_hbm, pos_ref[0, 0, 2 * i + 1], buf1, i, sem.at[1]).start()
        return c

    lax.fori_loop(0, tm, start, 0)

    def wait(i, c):
        _row_copy(y_hbm, 0, buf0, i, sem.at[0]).wait()
        _row_copy(y_hbm, 0, buf1, i, sem.at[1]).wait()
        return c

    lax.fori_loop(0, tm, wait, 0)
    moe = buf0[...] + buf1[...]
    o_ref[0] = _layer_norm(alpha * x1_ref[0] + g2_ref[0] * moe, lg_ref[...], lbias_ref[...])


def _combine(y_slots, pos, x1, g2, ln_g, ln_b, alpha, tm):
    b, t, d = x1.shape
    nt = t // tm
    per_row = g2.shape[1] != 1
    cond_block = (1, tm, d) if per_row else (1, 1, d)
    cond_map = (lambda bi, ti: (bi, ti, 0)) if per_row else (lambda bi, ti: (bi, 0, 0))
    kern = functools.partial(_combine_kernel, tm=tm, alpha=alpha)
    return pl.pallas_call(
        kern,
        out_shape=jax.ShapeDtypeStruct((b, t, d), F32),
        grid=(b, nt),
        in_specs=[pl.BlockSpec((1, 1, 2 * tm), lambda bi, ti: (bi * nt + ti, 0, 0), memory_space=pltpu.SMEM),
                  pl.BlockSpec(memory_space=pl.ANY),
                  pl.BlockSpec((1, tm, d), lambda bi, ti: (bi, ti, 0)),
                  pl.BlockSpec(cond_block, cond_map),
                  pl.BlockSpec((1, d), lambda bi, ti: (0, 0)),
                  pl.BlockSpec((1, d), lambda bi, ti: (0, 0))],
        out_specs=pl.BlockSpec((1, tm, d), lambda bi, ti: (bi, ti, 0)),
        scratch_shapes=[pltpu.VMEM((tm, d), F32), pltpu.VMEM((tm, d), F32), pltpu.SemaphoreType.DMA((2,))],
        compiler_params=_cparams(("arbitrary", "arbitrary")),
        name="combine",
    )(pos.reshape(b * nt, 1, 2 * tm), y_slots, x1, g2, ln_g.reshape(1, d), ln_b.reshape(1, d))


def _dispatch_plan(route, tm):
    n = route.shape[0]
    eid = route[:, 0:2].astype(jnp.int32).reshape(-1)
    gate = route[:, 2:4].reshape(-1)
    tok = jnp.arange(2 * n, dtype=jnp.int32) // 2
    order = jnp.argsort(eid, stable=True)
    sorted_e = eid[order]
    counts = jnp.zeros((N_EXPERTS,), jnp.int32).at[eid].add(1)
    padded = ((counts + tm - 1) // tm) * tm
    pad_end = jnp.cumsum(padded)
    pad_off = pad_end - padded
    off = jnp.cumsum(counts) - counts
    slot = pad_off[sorted_e] + (jnp.arange(2 * n, dtype=jnp.int32) - off[sorted_e])
    n_tiles = (2 * n) // tm + N_EXPERTS
    n_slots = n_tiles * tm
    slot_token = jnp.zeros((n_slots,), jnp.int32).at[slot].set(tok[order])
    slot_gate = jnp.zeros((n_slots,), F32).at[slot].set(gate[order])
    pos = jnp.zeros((2 * n,), jnp.int32).at[order].set(slot)
    tile_start = jnp.arange(n_tiles, dtype=jnp.int32) * tm
    tile_expert = jnp.minimum(jnp.searchsorted(pad_end, tile_start, side="right"), N_EXPERTS - 1).astype(jnp.int32)
    tile_valid = (tile_start < pad_end[-1]).astype(jnp.int32)
    return tile_expert, tile_valid, slot_token.reshape(n_tiles, 1, tm), slot_gate.reshape(n_slots, 1), pos


def kernel(x_prompt, x_sample, cache_k, cache_v, state_hgrn, page_table, c_prompt, c_sample, hg_lb_logits, w_ada, b_ada, w_in, hg_norm, sb_norm, sb_bias, w_o, ln1_g, ln1_b, w_gr, b_gr, w_er, b_er, w_gate, w_up, w_down, ln2_g, ln2_b):
    depth = w_ada.shape[0]
    assert depth == 1, "single-layer step"
    l = 0
    bp, tp, d = x_prompt.shape
    db, ds, _ = x_sample.shape
    heads = state_hgrn.shape[2]
    hw = heads * HEAD_DIM
    alpha = (2.0 * depth) ** 0.25
    n_phys, page = cache_k.shape[1], cache_k.shape[2]

    lower_bounds = jnp.cumsum(jax.nn.softmax(hg_lb_logits.astype(F32), axis=0), axis=0)
    lb = lower_bounds[l]

    n_c = bp + db
    c_rows = -(-n_c // SUBLANE) * SUBLANE
    c_all = jnp.concatenate([c_prompt, c_sample, jnp.zeros((c_rows - n_c, d), F32)], axis=0)
    ada = _ada(c_all, w_ada[l], b_ada[l])
    ada_p = ada[:bp].reshape(bp, 1, 6 * d)
    ada_s = jnp.repeat(ada[bp:bp + db], ds, axis=0).reshape(1, db * ds, 6 * d)

    w_in_bf = w_in[l].astype(BF16)
    w_o_bf = w_o[l].astype(BF16)
    wg, wu, wd = w_gate[l].astype(BF16), w_up[l].astype(BF16), w_down[l].astype(BF16)
    n_r = N_GROUPS + N_EXPERTS
    w_r = jnp.concatenate([w_gr[l], w_er[l].reshape(d, N_EXPERTS), jnp.zeros((d, LANE - n_r), F32)], axis=1)
    b_r = jnp.concatenate([b_gr[l], b_er[l].reshape(-1), jnp.zeros((LANE - n_r,), F32)]).reshape(1, LANE)

    def split(a):
        return [a[:, :, i * d:(i + 1) * d] for i in range(6)]

    def tail(o_h, o_s, x, g1, sc2, sh2, g2, tm):
        b, t, _ = x.shape
        x1, u2, route = _out_proj(o_h, o_s, x, g1, sc2, sh2, w_o_bf, ln1_g[l], ln1_b[l], w_r, b_r, alpha, tm)
        te, tv, slot_token, slot_gate, pos = _dispatch_plan(route.reshape(b * t, LANE), MOE_TM)
        y_slots = _moe(u2.reshape(b * t, d), te, tv, slot_token, slot_gate, wg, wu, wd)
        return _combine(y_slots, pos.reshape(b, t // tm, 1, 2 * tm), x1, g2, ln2_g[l], ln2_b[l], alpha, tm)

    sh1, sc1, g1, sh2, sc2, g2 = split(ada_p)
    proj_p = _in_proj(x_prompt, sc1, sh1, w_in_bf, 512)
    s0_p = jnp.zeros((bp, heads, HEAD_DIM, HEAD_DIM), F32)
    oh_p, s_p = _hgrn(proj_p, s0_p, lb, hg_norm[l], HG_CHUNK, tp)
    os_p = _sb_prompt(proj_p, sb_bias[l], sb_norm[l], heads, (4 * hw) // HEAD_DIM)
    y_p = tail(oh_p, os_p, x_prompt, g1, sc2, sh2, g2, ROW_TM)
    k_p = proj_p[:, :, 5 * hw:6 * hw].reshape(1, bp, tp, heads, HEAD_DIM)
    v_p = proj_p[:, :, 6 * hw:7 * hw].reshape(1, bp, tp, heads, HEAD_DIM)

    n_s = db * ds
    xs = x_sample.reshape(1, n_s, d)
    sh1, sc1, g1, sh2, sc2, g2 = split(ada_s)
    proj_s = _in_proj(xs, sc1, sh1, w_in_bf, n_s)
    rows = SUBLANE
    proj_s4 = proj_s.reshape(db, ds, -1)
    proj_s8 = jnp.pad(proj_s4, ((0, 0), (0, rows - ds), (0, 0)))
    oh_s8, s_s = _hgrn(proj_s8, state_hgrn[l], lb, hg_norm[l], rows, ds)
    os_s8 = _sb_sample(proj_s8[:, :, 4 * hw:5 * hw], proj_s8[:, :, 5 * hw:6 * hw], proj_s8[:, :, 6 * hw:7 * hw],
                       cache_k[l].reshape(n_phys, page, hw), cache_v[l].reshape(n_phys, page, hw),
                       page_table, sb_bias[l], sb_norm[l], heads, ds)
    oh_s = oh_s8[:, :ds].reshape(1, n_s, hw)
    os_s = os_s8[:, :ds].reshape(1, n_s, hw)
    y_s = tail(oh_s, os_s, xs, g1, sc2, sh2, g2, ROW_TM).reshape(db, ds, d)
    k_s = proj_s4[:, :, 5 * hw:6 * hw].reshape(1, db, ds, heads, HEAD_DIM)
    v_s = proj_s4[:, :, 6 * hw:7 * hw].reshape(1, db, ds, heads, HEAD_DIM)

    return (y_p, y_s, k_p, v_p, s_p[None], k_s, v_s, s_s[None])
```

```python
import functools
import math

import jax
import jax.numpy as jnp
from jax import lax
from jax.experimental import pallas as pl
from jax.experimental.pallas import tpu as pltpu

F32 = jnp.float32
BF16 = jnp.bfloat16
HIGHEST = lax.Precision.HIGHEST

LANE = 128
SUBLANE = 8
HEAD_DIM = 128
N_GROUPS = 4
EXPERTS_PER_GROUP = 8
N_EXPERTS = N_GROUPS * EXPERTS_PER_GROUP
LN_EPS = 1e-5
RMS_EPS = 1e-6
VMEM_LIMIT = 56 * 1024 * 1024
HG_CHUNK = 64
HG_SUB = SUBLANE
MOE_TM = 256
ROW_TM = 256


def _cparams(sem):
    return pltpu.CompilerParams(dimension_semantics=sem, vmem_limit_bytes=VMEM_LIMIT)


def _silu(x):
    return x * jax.nn.sigmoid(x)


def _dot(a, b):
    return jnp.dot(a, b, preferred_element_type=F32)


def _dot_nt(a, b):
    return lax.dot_general(a, b, (((1,), (1,)), ((), ())), preferred_element_type=F32)


def _dot_f32(a, b):
    return jnp.dot(a, b, preferred_element_type=F32, precision=HIGHEST)


def _ada_kernel(c_ref, w_ref, b_ref, o_ref):
    o_ref[...] = _dot_f32(_silu(c_ref[...]), w_ref[...]) + b_ref[...]


def _ada(c, w_ada, b_ada):
    rows, d = c.shape
    n = w_ada.shape[1]
    tn = 1024
    return pl.pallas_call(
        _ada_kernel,
        out_shape=jax.ShapeDtypeStruct((rows, n), F32),
        grid=(n // tn,),
        in_specs=[pl.BlockSpec((rows, d), lambda j: (0, 0)),
                  pl.BlockSpec((d, tn), lambda j: (0, j)),
                  pl.BlockSpec((1, tn), lambda j: (0, j))],
        out_specs=pl.BlockSpec((rows, tn), lambda j: (0, j)),
        compiler_params=_cparams(("arbitrary",)),
        name="ada",
    )(c, w_ada, b_ada.reshape(1, n))


def _in_proj_kernel(x_ref, sc_ref, sh_ref, w_ref, o_ref, u_ref):
    @pl.when(pl.program_id(2) == 0)
    def _():
        u_ref[...] = (x_ref[0] * (1.0 + sc_ref[0]) + sh_ref[0]).astype(BF16)

    o_ref[0] = _dot(u_ref[...], w_ref[...])


def _in_proj(x, sc, sh, w_bf16, tm):
    b, t, d = x.shape
    n = w_bf16.shape[1]
    tn = 1024
    per_row = sc.shape[1] != 1
    cond_block = (1, tm, d) if per_row else (1, 1, d)
    cond_map = (lambda bi, ti, ni: (bi, ti, 0)) if per_row else (lambda bi, ti, ni: (bi, 0, 0))
    return pl.pallas_call(
        _in_proj_kernel,
        out_shape=jax.ShapeDtypeStruct((b, t, n), F32),
        grid=(b, t // tm, n // tn),
        in_specs=[pl.BlockSpec((1, tm, d), lambda bi, ti, ni: (bi, ti, 0)),
                  pl.BlockSpec(cond_block, cond_map),
                  pl.BlockSpec(cond_block, cond_map),
                  pl.BlockSpec((d, tn), lambda bi, ti, ni: (0, ni))],
        out_specs=pl.BlockSpec((1, tm, tn), lambda bi, ti, ni: (bi, ti, ni)),
        scratch_shapes=[pltpu.VMEM((tm, d), BF16)],
        compiler_params=_cparams(("arbitrary", "arbitrary", "arbitrary")),
        name="in_proj",
    )(x, sc, sh, w_bf16)


def _hgrn_levels(chunk):
    levels = []
    m = HG_SUB
    while m < chunk:
        levels.append(m)
        m *= 2
    return levels


def _hgrn_static(chunk):
    t = jnp.arange(chunk)
    mats = [(t[None, :] <= t[:, None])]
    for m in _hgrn_levels(chunk):
        boundary = (t // (2 * m)) * (2 * m) + m - 1
        mats.append(t[None, :] <= boundary[:, None])
    return jnp.concatenate(mats, axis=0).astype(BF16)


def _hgrn_kernel(p_ref, s0_ref, lb_ref, gn_ref, cum_ref, o_ref, sfin_ref, s_ref, *, chunk, heads, t_valid):
    ci = pl.program_id(1)

    @pl.when(ci == 0)
    def _():
        s_ref[...] = s0_ref[0]

    hw = heads * HEAD_DIM
    row = lax.broadcasted_iota(jnp.int32, (chunk, 1), 0)
    valid = (ci * chunk + row) < t_valid
    rr = lax.broadcasted_iota(jnp.int32, (chunk, chunk), 0)
    cc = lax.broadcasted_iota(jnp.int32, (chunk, chunk), 1)
    nsub = chunk // HG_SUB
    sub_row = lax.broadcasted_iota(jnp.int32, (nsub, HG_SUB, 1), 1)
    levels = _hgrn_levels(chunk)
    cum = cum_ref[...]

    for h in range(heads):
        lo = h * HEAD_DIM
        hq = p_ref[0, :, lo:lo + HEAD_DIM]
        hf = p_ref[0, :, hw + lo:hw + lo + HEAD_DIM]
        v = p_ref[0, :, 2 * hw + lo:2 * hw + lo + HEAD_DIM]
        hg = p_ref[0, :, 3 * hw + lo:3 * hw + lo + HEAD_DIM]
        lb = lb_ref[:, lo:lo + HEAD_DIM]
        f = lb + (1.0 - lb) * jax.nn.sigmoid(hf)
        g = jnp.where(valid, jnp.log(f), 0.0)
        k = jnp.where(valid, 1.0 - f, 0.0)
        q = _silu(hq)

        g_hi = g.astype(BF16)
        g_r = g - g_hi.astype(F32)
        g_mid = g_r.astype(BF16)
        g_lo = (g_r - g_mid.astype(F32)).astype(BF16)
        sums = _dot(cum, g_hi) + _dot(cum, g_mid) + _dot(cum, g_lo)
        b = sums[0:chunk]
        b_last = b[chunk - 1:chunk, :]
        s_old = s_ref[h]

        o = _dot((q * jnp.exp(b)).astype(BF16), s_old.astype(BF16))

        a = jnp.zeros((chunk, chunk), F32)
        for li, m in enumerate(levels):
            bm = sums[(li + 1) * chunk:(li + 2) * chunk]
            upper = (row % (2 * m)) >= m
            qs = jnp.where(upper, q * jnp.exp(jnp.where(upper, b - bm, 0.0)), 0.0)
            ks = jnp.where(upper, 0.0, k * jnp.exp(jnp.where(upper, 0.0, bm - b)))
            same = (rr // (2 * m)) == (cc // (2 * m))
            a = a + jnp.where(same, _dot_nt(qs.astype(BF16), ks.astype(BF16)), 0.0)

        q3 = q.reshape(nsub, HG_SUB, HEAD_DIM)
        k3 = k.reshape(nsub, HG_SUB, HEAD_DIM)
        b3 = b.reshape(nsub, HG_SUB, HEAD_DIM)
        for j in range(HG_SUB):
            kj = k3[:, j:j + 1, :]
            bj = b3[:, j:j + 1, :]
            e = jnp.exp(jnp.where(sub_row >= j, b3 - bj, 0.0))
            col = jnp.sum(q3 * kj * e, axis=-1, keepdims=True).reshape(chunk, 1)
            hit = (cc == (rr // HG_SUB) * HG_SUB + j) & ((rr % HG_SUB) >= j)
            a = a + jnp.where(hit, col, 0.0)

        o = o + _dot(a.astype(BF16), v.astype(BF16))

        kd = k * jnp.exp(b_last - b)
        if chunk < HEAD_DIM:
            pad = jnp.zeros((HEAD_DIM - chunk, HEAD_DIM), F32)
            kd = jnp.concatenate([kd, pad], axis=0)
            vp = jnp.concatenate([v, pad], axis=0)
        else:
            vp = v
        decay_col = jnp.broadcast_to(jnp.exp(b_last), (HEAD_DIM, HEAD_DIM)).T
        s_ref[h] = decay_col * s_old + _dot(kd.T.astype(BF16), vp.astype(BF16))

        ms = jnp.mean(o * o, axis=-1, keepdims=True)
        on = o * lax.rsqrt(ms + RMS_EPS) * gn_ref[...]
        o_ref[0, :, lo:lo + HEAD_DIM] = (on * _silu(hg)).astype(o_ref.dtype)

    @pl.when(ci == pl.num_programs(1) - 1)
    def _():
        sfin_ref[0] = s_ref[...]


def _hgrn(proj, s0, lb, hg_norm, chunk, t_valid):
    b, t, _ = proj.shape
    heads = s0.shape[1]
    hw = heads * HEAD_DIM
    cum = _hgrn_static(chunk)
    kern = functools.partial(_hgrn_kernel, chunk=chunk, heads=heads, t_valid=t_valid)
    return pl.pallas_call(
        kern,
        out_shape=(jax.ShapeDtypeStruct((b, t, hw), BF16),
                   jax.ShapeDtypeStruct(s0.shape, F32)),
        grid=(b, t // chunk),
        in_specs=[pl.BlockSpec((1, chunk, 4 * hw), lambda bi, ci: (bi, ci, 0)),
                  pl.BlockSpec((1, heads, HEAD_DIM, HEAD_DIM), lambda bi, ci: (bi, 0, 0, 0)),
                  pl.BlockSpec((1, hw), lambda bi, ci: (0, 0)),
                  pl.BlockSpec((1, HEAD_DIM), lambda bi, ci: (0, 0)),
                  pl.BlockSpec(cum.shape, lambda bi, ci: (0, 0))],
        out_specs=(pl.BlockSpec((1, chunk, hw), lambda bi, ci: (bi, ci, 0)),
                   pl.BlockSpec((1, heads, HEAD_DIM, HEAD_DIM), lambda bi, ci: (bi, 0, 0, 0))),
        scratch_shapes=[pltpu.VMEM((heads, HEAD_DIM, HEAD_DIM), F32)],
        compiler_params=_cparams(("arbitrary", "arbitrary")),
        name="hgrn",
    )(proj, s0, lb.reshape(1, hw), hg_norm.reshape(1, HEAD_DIM), cum)


def _sb_block(q_bf, k, v, bias, mask, carry, acc, upper_bf):
    z = _dot_nt(q_bf, k.astype(BF16)) + bias
    sp = jnp.maximum(z, 0.0) + jnp.log(1.0 + jnp.exp(-jnp.abs(z)))
    lp = -sp if mask is None else jnp.where(mask, -sp, 0.0)
    lp_hi = lp.astype(BF16)
    lp_lo = (lp - lp_hi.astype(F32)).astype(BF16)
    cs = _dot(lp_hi, upper_bf) + _dot(lp_lo, upper_bf)
    logw = ((z - sp) + (cs - lp)) + carry
    w = jnp.exp(logw)
    if mask is not None:
        w = jnp.where(mask, w, 0.0)
    acc = acc + _dot(w.astype(BF16), v.astype(BF16))
    carry = carry + cs[:, 0:1]
    return carry, acc


def _upper_ones(n):
    r = lax.broadcasted_iota(jnp.int32, (n, n), 0)
    c = lax.broadcasted_iota(jnp.int32, (n, n), 1)
    return (r >= c).astype(BF16)


def _sb_prompt_kernel(bias_ref, q_ref, k_ref, v_ref, gn_ref, o_ref, *, qb, scale, unroll):
    h = pl.program_id(1)
    i = pl.program_id(2)
    bias = bias_ref[h]
    q_bf = (q_ref[0] * scale).astype(BF16)
    upper = _upper_ones(qb)
    q_pos = i * qb + lax.broadcasted_iota(jnp.int32, (qb, qb), 0)
    col = lax.broadcasted_iota(jnp.int32, (qb, qb), 1)

    def block(j, mask, state):
        start = pl.multiple_of(j * qb, qb)
        k = k_ref[0, pl.ds(start, qb), :]
        v = v_ref[0, pl.ds(start, qb), :]
        return _sb_block(q_bf, k, v, bias, mask, state[0], state[1], upper)

    state = block(i, (i * qb + col) < q_pos, (jnp.zeros((qb, 1), F32), jnp.zeros((qb, HEAD_DIM), F32)))
    n_groups = i // unroll

    def group(gi, state):
        for u in range(unroll):
            state = block(i - 1 - (gi * unroll + u), None, state)
        return state

    state = lax.fori_loop(0, n_groups, group, state)
    first = i - 1 - n_groups * unroll
    _, acc = lax.fori_loop(0, first + 1, lambda s, st: block(first - s, None, st), state)
    ms = jnp.mean(acc * acc, axis=-1, keepdims=True)
    o_ref[0] = (acc * lax.rsqrt(ms + RMS_EPS) * gn_ref[...]).astype(o_ref.dtype)


def _sb_prompt(proj, sb_bias, sb_norm, heads, col0):
    b, t, _ = proj.shape
    qb = 128
    kern = functools.partial(_sb_prompt_kernel, qb=qb, scale=1.0 / math.sqrt(HEAD_DIM), unroll=4)
    grid_spec = pltpu.PrefetchScalarGridSpec(
        num_scalar_prefetch=0,
        grid=(b, heads, t // qb),
        in_specs=[pl.BlockSpec(memory_space=pltpu.SMEM),
                  pl.BlockSpec((1, qb, HEAD_DIM), lambda bi, hi, qi: (bi, qi, col0 + hi)),
                  pl.BlockSpec((1, t, HEAD_DIM), lambda bi, hi, qi: (bi, 0, col0 + heads + hi)),
                  pl.BlockSpec((1, t, HEAD_DIM), lambda bi, hi, qi: (bi, 0, col0 + 2 * heads + hi)),
                  pl.BlockSpec((1, HEAD_DIM), lambda bi, hi, qi: (0, 0))],
        out_specs=pl.BlockSpec((1, qb, HEAD_DIM), lambda bi, hi, qi: (bi, qi, hi)),
    )
    return pl.pallas_call(
        kern,
        out_shape=jax.ShapeDtypeStruct((b, t, heads * HEAD_DIM), BF16),
        grid_spec=grid_spec,
        compiler_params=_cparams(("arbitrary", "arbitrary", "arbitrary")),
        name="sb_prompt",
    )(sb_bias, proj, proj, proj, sb_norm.reshape(1, HEAD_DIM))


def _sb_heads_block(q_bf, k_full, v_full, bias_col, mask, carry, acc, upper_bf, heads):
    rows = q_bf[0].shape[0]
    k_bf = k_full.astype(BF16)
    v_bf = v_full.astype(BF16)
    z = jnp.concatenate([_dot_nt(q_bf[h], k_bf[:, h * HEAD_DIM:(h + 1) * HEAD_DIM]) for h in range(heads)], axis=0)
    z = z + bias_col
    sp = jnp.maximum(z, 0.0) + jnp.log(1.0 + jnp.exp(-jnp.abs(z)))
    lp = -sp if mask is None else jnp.where(mask, -sp, 0.0)
    lp_hi = lp.astype(BF16)
    lp_lo = (lp - lp_hi.astype(F32)).astype(BF16)
    cs = _dot(lp_hi, upper_bf) + _dot(lp_lo, upper_bf)
    w = jnp.exp(((z - sp) + (cs - lp)) + carry)
    if mask is not None:
        w = jnp.where(mask, w, 0.0)
    w_bf = w.astype(BF16)
    pv = jnp.concatenate([_dot(w_bf[h * rows:(h + 1) * rows], v_bf[:, h * HEAD_DIM:(h + 1) * HEAD_DIM])
                          for h in range(heads)], axis=0)
    return carry + cs[:, 0:1], acc + pv


def _sb_sample_kernel(pt_ref, q_ref, ko_ref, vo_ref, *rest, heads, s_valid, scale, pages_per_step):
    kc_refs = rest[:pages_per_step]
    vc_refs = rest[pages_per_step:2 * pages_per_step]
    bias_ref, gn_ref, o_ref, carry_ref, acc_ref = rest[2 * pages_per_step:]
    p = pl.program_id(1)
    rows = q_ref.shape[1]
    page = kc_refs[0].shape[1]
    upper = _upper_ones(page)
    bias_col = bias_ref[...]
    q_bf = [(q_ref[0, :, h * HEAD_DIM:(h + 1) * HEAD_DIM] * scale).astype(BF16) for h in range(heads)]

    @pl.when(p == 0)
    def _():
        r = lax.broadcasted_iota(jnp.int32, (heads * rows, page), 0) % rows
        c = lax.broadcasted_iota(jnp.int32, (heads * rows, page), 1)
        mask = (c < r) & (c < s_valid)
        pad = jnp.zeros((page - rows, heads * HEAD_DIM), F32)
        k = jnp.concatenate([ko_ref[0], pad], axis=0)
        v = jnp.concatenate([vo_ref[0], pad], axis=0)
        carry, acc = _sb_heads_block(q_bf, k, v, bias_col, mask, jnp.zeros((heads * rows, 1), F32),
                                     jnp.zeros((heads * rows, HEAD_DIM), F32), upper, heads)
        carry_ref[...] = jnp.broadcast_to(carry, carry_ref.shape)
        acc_ref[...] = acc

    carry = carry_ref[:, 0:1]
    acc = acc_ref[...]
    for kc_ref, vc_ref in zip(kc_refs, vc_refs):
        carry, acc = _sb_heads_block(q_bf, kc_ref[0], vc_ref[0], bias_col, None, carry, acc, upper, heads)
    carry_ref[...] = jnp.broadcast_to(carry, carry_ref.shape)
    acc_ref[...] = acc

    @pl.when(p == pl.num_programs(1) - 1)
    def _():
        ms = jnp.mean(acc * acc, axis=-1, keepdims=True)
        on = (acc * lax.rsqrt(ms + RMS_EPS) * gn_ref[...]).astype(o_ref.dtype)
        for h in range(heads):
            o_ref[0, :, h * HEAD_DIM:(h + 1) * HEAD_DIM] = on[h * rows:(h + 1) * rows]


def _sb_sample(q, k_own, v_own, cache_k, cache_v, page_ids, sb_bias, sb_norm, heads, s_valid):
    db, rows, hw = q.shape
    n_pages = page_ids.shape[1]
    page = cache_k.shape[1]
    pps = math.gcd(4, n_pages)
    kern = functools.partial(_sb_sample_kernel, heads=heads, s_valid=s_valid, scale=1.0 / math.sqrt(HEAD_DIM),
                             pages_per_step=pps)

    def page_map(j):
        return lambda bi, pi, pt: (pt[bi * n_pages + (n_pages - 1 - (pi * pps + j))], 0, 0)

    own_map = lambda bi, pi, pt: (bi, 0, 0)
    fixed = lambda bi, pi, pt: (0, 0)
    page_specs = [pl.BlockSpec((1, page, hw), page_map(j)) for j in range(pps)]
    grid_spec = pltpu.PrefetchScalarGridSpec(
        num_scalar_prefetch=1,
        grid=(db, n_pages // pps),
        in_specs=[pl.BlockSpec((1, rows, hw), own_map),
                  pl.BlockSpec((1, rows, hw), own_map),
                  pl.BlockSpec((1, rows, hw), own_map)] + page_specs + page_specs +
                 [pl.BlockSpec((heads * rows, 1), fixed),
                  pl.BlockSpec((1, HEAD_DIM), fixed)],
        out_specs=pl.BlockSpec((1, rows, hw), own_map),
        scratch_shapes=[pltpu.VMEM((heads * rows, HEAD_DIM), F32),
                        pltpu.VMEM((heads * rows, HEAD_DIM), F32)],
    )
    bias_col = jnp.repeat(sb_bias.astype(F32), rows).reshape(heads * rows, 1)
    return pl.pallas_call(
        kern,
        out_shape=jax.ShapeDtypeStruct((db, rows, hw), BF16),
        grid_spec=grid_spec,
        compiler_params=_cparams(("arbitrary", "arbitrary")),
        name="sb_sample",
    )(page_ids.reshape(-1), q, k_own, v_own, *([cache_k] * pps), *([cache_v] * pps), bias_col,
      sb_norm.reshape(1, HEAD_DIM))


def _layer_norm(y, g, b):
    mu = jnp.mean(y, axis=-1, keepdims=True)
    yc = y - mu
    var = jnp.mean(yc * yc, axis=-1, keepdims=True)
    return yc * lax.rsqrt(var + LN_EPS) * g + b


def _out_proj_kernel(oh_ref, os_ref, x_ref, g1_ref, sc2_ref, sh2_ref, wt_ref, wb_ref, lg_ref, lbias_ref,
                     wr_ref, br_ref, x1_ref, u2_ref, route_ref, *, alpha):
    mix = _dot(oh_ref[0], wt_ref[...]) + _dot(os_ref[0], wb_ref[...])
    x1 = _layer_norm(alpha * x_ref[0] + g1_ref[0] * mix, lg_ref[...], lbias_ref[...])
    x1_ref[0] = x1
    u2 = x1 * (1.0 + sc2_ref[0]) + sh2_ref[0]
    u2_ref[0] = u2

    logits = _dot_f32(u2, wr_ref[...]) + br_ref[...]
    lane = lax.broadcasted_iota(jnp.int32, logits.shape, 1)
    neg = -jnp.inf
    gl = jnp.where(lane < N_GROUPS, logits, neg)
    gmax = jnp.max(gl, axis=-1, keepdims=True)
    gidx = jnp.min(jnp.where(gl == gmax, lane, LANE), axis=-1, keepdims=True)
    g_w = 1.0 / jnp.sum(jnp.exp(gl - gmax), axis=-1, keepdims=True)
    in_group = (lane >= N_GROUPS) & (lane < N_GROUPS + N_EXPERTS) & \
               (((lane - N_GROUPS) // EXPERTS_PER_GROUP) == gidx)
    el = jnp.where(in_group, logits, neg)
    v1 = jnp.max(el, axis=-1, keepdims=True)
    i1 = jnp.min(jnp.where(el == v1, lane, LANE), axis=-1, keepdims=True)
    el2 = jnp.where(lane == i1, neg, el)
    v2 = jnp.max(el2, axis=-1, keepdims=True)
    i2 = jnp.min(jnp.where(el2 == v2, lane, LANE), axis=-1, keepdims=True)
    e21 = jnp.exp(v2 - v1)
    p1 = 1.0 / (1.0 + e21)
    p2 = e21 * p1
    route = jnp.where(lane == 0, (i1 - N_GROUPS).astype(F32),
            jnp.where(lane == 1, (i2 - N_GROUPS).astype(F32),
            jnp.where(lane == 2, g_w * p1,
            jnp.where(lane == 3, g_w * p2, 0.0))))
    route_ref[0] = route


def _out_proj(o_h, o_s, x, g1, sc2, sh2, w_o_bf16, ln_g, ln_b, w_r, b_r, alpha, tm):
    b, t, d = x.shape
    hw = o_h.shape[2]
    per_row = g1.shape[1] != 1
    cond_block = (1, tm, d) if per_row else (1, 1, d)
    cond_map = (lambda bi, ti: (bi, ti, 0)) if per_row else (lambda bi, ti: (bi, 0, 0))
    row_map = lambda bi, ti: (bi, ti, 0)
    fixed = lambda bi, ti: (0, 0)
    kern = functools.partial(_out_proj_kernel, alpha=alpha)
    return pl.pallas_call(
        kern,
        out_shape=(jax.ShapeDtypeStruct((b, t, d), F32),
                   jax.ShapeDtypeStruct((b, t, d), F32),
                   jax.ShapeDtypeStruct((b, t, LANE), F32)),
        grid=(b, t // tm),
        in_specs=[pl.BlockSpec((1, tm, hw), row_map),
                  pl.BlockSpec((1, tm, hw), row_map),
                  pl.BlockSpec((1, tm, d), row_map),
                  pl.BlockSpec(cond_block, cond_map),
                  pl.BlockSpec(cond_block, cond_map),
                  pl.BlockSpec(cond_block, cond_map),
                  pl.BlockSpec((hw, d), lambda bi, ti: (0, 0)),
                  pl.BlockSpec((hw, d), lambda bi, ti: (1, 0)),
                  pl.BlockSpec((1, d), fixed),
                  pl.BlockSpec((1, d), fixed),
                  pl.BlockSpec((d, LANE), fixed),
                  pl.BlockSpec((1, LANE), fixed)],
        out_specs=(pl.BlockSpec((1, tm, d), row_map),
                   pl.BlockSpec((1, tm, d), row_map),
                   pl.BlockSpec((1, tm, LANE), row_map)),
        compiler_params=_cparams(("arbitrary", "arbitrary")),
        name="out_proj",
    )(o_h, o_s, x, g1, sc2, sh2, w_o_bf16, w_o_bf16, ln_g.reshape(1, d), ln_b.reshape(1, d), w_r, b_r)


def _row_copy(src_hbm, row, dst, i, sem):
    return pltpu.make_async_copy(src_hbm.at[pl.ds(row, 1)], dst.at[pl.ds(i, 1)], sem)


def _wait_rows(src_hbm, dst, sem):
    pltpu.make_async_copy(src_hbm.at[pl.ds(0, dst.shape[0])], dst, sem).wait()


def _moe_kernel(te_ref, tv_ref, tok_ref, u_hbm, wg_ref, wu_ref, wd_ref, y_ref, x_buf, sem, *, tm):
    t = pl.program_id(0)

    @pl.when(tv_ref[t] != 0)
    def _():
        def start(i, c):
            _row_copy(u_hbm, tok_ref[0, 0, i], x_buf, i, sem.at[0]).start()
            return c

        lax.fori_loop(0, tm, start, 0, unroll=8)
        _wait_rows(u_hbm, x_buf, sem.at[0])
        x = x_buf[...].astype(BF16)
        hid = _silu(_dot(x, wg_ref[0])) * _dot(x, wu_ref[0])
        y_ref[...] = _dot(hid.astype(BF16), wd_ref[0])

    @pl.when(tv_ref[t] == 0)
    def _():
        y_ref[...] = jnp.zeros_like(y_ref)


def _moe(u2, tile_expert, tile_valid, slot_token, wg, wu, wd):
    n, d = u2.shape
    tm = MOE_TM
    n_tiles = slot_token.shape[0]
    ff = wg.shape[2]
    kern = functools.partial(_moe_kernel, tm=tm)
    grid_spec = pltpu.PrefetchScalarGridSpec(
        num_scalar_prefetch=2,
        grid=(n_tiles,),
        in_specs=[pl.BlockSpec((1, 1, tm), lambda t, te, tv: (t, 0, 0), memory_space=pltpu.SMEM),
                  pl.BlockSpec(memory_space=pl.ANY),
                  pl.BlockSpec((1, d, ff), lambda t, te, tv: (te[t], 0, 0)),
                  pl.BlockSpec((1, d, ff), lambda t, te, tv: (te[t], 0, 0)),
                  pl.BlockSpec((1, ff, d), lambda t, te, tv: (te[t], 0, 0))],
        out_specs=pl.BlockSpec((tm, d), lambda t, te, tv: (t, 0)),
        scratch_shapes=[pltpu.VMEM((tm, d), F32), pltpu.SemaphoreType.DMA((1,))],
    )
    return pl.pallas_call(
        kern,
        out_shape=jax.ShapeDtypeStruct((n_tiles * tm, d), F32),
        grid_spec=grid_spec,
        compiler_params=_cparams(("arbitrary",)),
        name="moe",
    )(tile_expert, tile_valid, slot_token, u2, wg, wu, wd)


def _combine_kernel(pos_ref, y_hbm, x1_ref, route_ref, g2_ref, lg_ref, lbias_ref, o_ref, buf0, buf1, sem,
                    *, tm, alpha):
    def start(i, c):
        _row_copy(y_hbm, pos_ref[0, 0, 2 * i], buf0, i, sem.at[0]).start()
        _row_copy(y_hbm, pos_ref[0, 0, 2 * i + 1], buf1, i, sem.at[1]).start()
        return c

    lax.fori_loop(0, tm, start, 0, unroll=8)
    _wait_rows(y_hbm, buf0, sem.at[0])
    _wait_rows(y_hbm, buf1, sem.at[1])
    route = route_ref[0]
    moe = route[:, 2:3] * buf0[...] + route[:, 3:4] * buf1[...]
    o_ref[0] = _layer_norm(alpha * x1_ref[0] + g2_ref[0] * moe, lg_ref[...], lbias_ref[...])


def _combine(y_slots, pos, x1, route, g2, ln_g, ln_b, alpha, tm):
    b, t, d = x1.shape
    nt = t // tm
    per_row = g2.shape[1] != 1
    cond_block = (1, tm, d) if per_row else (1, 1, d)
    cond_map = (lambda bi, ti: (bi, ti, 0)) if per_row else (lambda bi, ti: (bi, 0, 0))
    kern = functools.partial(_combine_kernel, tm=tm, alpha=alpha)
    return pl.pallas_call(
        kern,
        out_shape=jax.ShapeDtypeStruct((b, t, d), F32),
        grid=(b, nt),
        in_specs=[pl.BlockSpec((1, 1, 2 * tm), lambda bi, ti: (bi * nt + ti, 0, 0), memory_space=pltpu.SMEM),
                  pl.BlockSpec(memory_space=pl.ANY),
                  pl.BlockSpec((1, tm, d), lambda bi, ti: (bi, ti, 0)),
                  pl.BlockSpec((1, tm, LANE), lambda bi, ti: (bi, ti, 0)),
                  pl.BlockSpec(cond_block, cond_map),
                  pl.BlockSpec((1, d), lambda bi, ti: (0, 0)),
                  pl.BlockSpec((1, d), lambda bi, ti: (0, 0))],
        out_specs=pl.BlockSpec((1, tm, d), lambda bi, ti: (bi, ti, 0)),
        scratch_shapes=[pltpu.VMEM((tm, d), F32), pltpu.VMEM((tm, d), F32), pltpu.SemaphoreType.DMA((2,))],
        compiler_params=_cparams(("arbitrary", "arbitrary")),
        name="combine",
    )(pos.reshape(b * nt, 1, 2 * tm), y_slots, x1, route, g2, ln_g.reshape(1, d), ln_b.reshape(1, d))


def _dispatch_plan(route, tm):
    n = route.shape[0]
    eid = route[:, 0:2].astype(jnp.int32).reshape(-1)
    onehot = (eid[:, None] == jnp.arange(N_EXPERTS, dtype=jnp.int32)[None, :]).astype(jnp.int32)
    running = jnp.cumsum(onehot, axis=0)
    counts = running[-1]
    padded = ((counts + tm - 1) // tm) * tm
    pad_end = jnp.cumsum(padded)
    pad_off = pad_end - padded
    slot = jnp.sum(onehot * (running - 1 + pad_off[None, :]), axis=1)
    n_tiles = (2 * n) // tm + N_EXPERTS
    n_slots = n_tiles * tm
    tok = jnp.arange(2 * n, dtype=jnp.int32) // 2
    slot_token = jnp.zeros((n_slots,), jnp.int32).at[slot].set(tok, unique_indices=True)
    tile_start = jnp.arange(n_tiles, dtype=jnp.int32) * tm
    tile_expert = jnp.minimum(jnp.sum((tile_start[:, None] >= pad_end[None, :]).astype(jnp.int32), axis=1),
                              N_EXPERTS - 1)
    tile_valid = (tile_start < pad_end[-1]).astype(jnp.int32)
    return tile_expert, tile_valid, slot_token.reshape(n_tiles, 1, tm), slot


def kernel(x_prompt, x_sample, cache_k, cache_v, state_hgrn, page_table, c_prompt, c_sample, hg_lb_logits, w_ada, b_ada, w_in, hg_norm, sb_norm, sb_bias, w_o, ln1_g, ln1_b, w_gr, b_gr, w_er, b_er, w_gate, w_up, w_down, ln2_g, ln2_b):
    depth = w_ada.shape[0]
    assert depth == 1, "single-layer step"
    bp, tp, d = x_prompt.shape
    db, ds, _ = x_sample.shape
    heads = state_hgrn.shape[2]
    hw = heads * HEAD_DIM
    alpha = (2.0 * depth) ** 0.25
    n_phys, page = cache_k.shape[1], cache_k.shape[2]

    def layer0(a):
        return a.reshape(a.shape[1:])

    lower_bounds = jnp.cumsum(jax.nn.softmax(hg_lb_logits.astype(F32), axis=0), axis=0)
    lb = lower_bounds[0]
    hg_norm, sb_norm, sb_bias = layer0(hg_norm), layer0(sb_norm), layer0(sb_bias)
    ln1_g, ln1_b, ln2_g, ln2_b = layer0(ln1_g), layer0(ln1_b), layer0(ln2_g), layer0(ln2_b)

    n_c = bp + db
    c_rows = -(-n_c // SUBLANE) * SUBLANE
    c_all = jnp.concatenate([c_prompt, c_sample, jnp.zeros((c_rows - n_c, d), F32)], axis=0)
    ada = _ada(c_all, layer0(w_ada), layer0(b_ada))
    ada_p = ada[:bp].reshape(bp, 1, 6 * d)
    ada_s = jnp.repeat(ada[bp:bp + db], ds, axis=0).reshape(1, db * ds, 6 * d)

    w_in_bf = layer0(w_in).astype(BF16)
    w_o_bf = layer0(w_o).astype(BF16)
    wg, wu, wd = layer0(w_gate).astype(BF16), layer0(w_up).astype(BF16), layer0(w_down).astype(BF16)
    n_r = N_GROUPS + N_EXPERTS
    w_r = jnp.concatenate([layer0(w_gr), w_er.reshape(d, N_EXPERTS), jnp.zeros((d, LANE - n_r), F32)], axis=1)
    b_r = jnp.concatenate([b_gr.reshape(-1), b_er.reshape(-1), jnp.zeros((LANE - n_r,), F32)]).reshape(1, LANE)

    def split(a):
        return [a[:, :, i * d:(i + 1) * d] for i in range(6)]

    def tail(o_h, o_s, x, g1, sc2, sh2, g2, tm):
        b, t, _ = x.shape
        x1, u2, route = _out_proj(o_h, o_s, x, g1, sc2, sh2, w_o_bf, ln1_g, ln1_b, w_r, b_r, alpha, tm)
        te, tv, slot_token, pos = _dispatch_plan(route.reshape(b * t, LANE), MOE_TM)
        y_slots = _moe(u2.reshape(b * t, d), te, tv, slot_token, wg, wu, wd)
        return _combine(y_slots, pos.reshape(b, t // tm, 1, 2 * tm), x1, route, g2, ln2_g, ln2_b, alpha, tm)

    sh1, sc1, g1, sh2, sc2, g2 = split(ada_p)
    proj_p = _in_proj(x_prompt, sc1, sh1, w_in_bf, 512)
    s0_p = jnp.zeros((bp, heads, HEAD_DIM, HEAD_DIM), F32)
    oh_p, s_p = _hgrn(proj_p, s0_p, lb, hg_norm, HG_CHUNK, tp)
    os_p = _sb_prompt(proj_p, sb_bias, sb_norm, heads, (4 * hw) // HEAD_DIM)
    y_p = tail(oh_p, os_p, x_prompt, g1, sc2, sh2, g2, ROW_TM)
    k_p = proj_p[:, :, 5 * hw:6 * hw].reshape(1, bp, tp, heads, HEAD_DIM)
    v_p = proj_p[:, :, 6 * hw:7 * hw].reshape(1, bp, tp, heads, HEAD_DIM)

    n_s = db * ds
    xs = x_sample.reshape(1, n_s, d)
    sh1, sc1, g1, sh2, sc2, g2 = split(ada_s)
    proj_s = _in_proj(xs, sc1, sh1, w_in_bf, n_s)
    rows = SUBLANE
    proj_s4 = proj_s.reshape(db, ds, -1)
    proj_s8 = jnp.pad(proj_s4, ((0, 0), (0, rows - ds), (0, 0)))
    oh_s8, s_s = _hgrn(proj_s8, layer0(state_hgrn), lb, hg_norm, rows, ds)
    os_s8 = _sb_sample(proj_s8[:, :, 4 * hw:5 * hw], proj_s8[:, :, 5 * hw:6 * hw], proj_s8[:, :, 6 * hw:7 * hw],
                       cache_k.reshape(n_phys, page, hw), cache_v.reshape(n_phys, page, hw),
                       page_table, sb_bias, sb_norm, heads, ds)
    oh_s = oh_s8[:, :ds].reshape(1, n_s, hw)
    os_s = os_s8[:, :ds].reshape(1, n_s, hw)
    y_s = tail(oh_s, os_s, xs, g1, sc2, sh2, g2, ROW_TM).reshape(db, ds, d)
    k_s = proj_s4[:, :, 5 * hw:6 * hw].reshape(1, db, ds, heads, HEAD_DIM)
    v_s = proj_s4[:, :, 6 * hw:7 * hw].reshape(1, db, ds, heads, HEAD_DIM)

    return (y_p, y_s, k_p, v_p, s_p[None], k_s, v_s, s_s[None])
```

```python
import functools
import math

import jax
import jax.numpy as jnp
from jax import lax
from jax.experimental import pallas as pl
from jax.experimental.pallas import tpu as pltpu

F32 = jnp.float32
BF16 = jnp.bfloat16
HIGHEST = lax.Precision.HIGHEST

LANE = 128
SUBLANE = 8
HEAD_DIM = 128
N_GROUPS = 4
EXPERTS_PER_GROUP = 8
N_EXPERTS = N_GROUPS * EXPERTS_PER_GROUP
LN_EPS = 1e-5
RMS_EPS = 1e-6
VMEM_LIMIT = 56 * 1024 * 1024
HG_CHUNK = 64
HG_SUB = SUBLANE
MOE_TM = 256
ROW_TM = 256


def _cparams(sem):
    return pltpu.CompilerParams(dimension_semantics=sem, vmem_limit_bytes=VMEM_LIMIT)


def _silu(x):
    return x * jax.nn.sigmoid(x)


def _dot(a, b):
    return jnp.dot(a, b, preferred_element_type=F32)


def _dot_nt(a, b):
    return lax.dot_general(a, b, (((1,), (1,)), ((), ())), preferred_element_type=F32)


def _dot_f32(a, b):
    return jnp.dot(a, b, preferred_element_type=F32, precision=HIGHEST)


def _ada_kernel(c_ref, w_ref, b_ref, o_ref):
    o_ref[...] = _dot_f32(_silu(c_ref[...]), w_ref[...]) + b_ref[...]


def _ada(c, w_ada, b_ada):
    rows, d = c.shape
    n = w_ada.shape[1]
    tn = 1024
    return pl.pallas_call(
        _ada_kernel,
        out_shape=jax.ShapeDtypeStruct((rows, n), F32),
        grid=(n // tn,),
        in_specs=[pl.BlockSpec((rows, d), lambda j: (0, 0)),
                  pl.BlockSpec((d, tn), lambda j: (0, j)),
                  pl.BlockSpec((1, tn), lambda j: (0, j))],
        out_specs=pl.BlockSpec((rows, tn), lambda j: (0, j)),
        compiler_params=_cparams(("arbitrary",)),
        name="ada",
    )(c, w_ada, b_ada.reshape(1, n))


def _in_proj_kernel(x_ref, sc_ref, sh_ref, w_ref, o_ref, u_ref):
    @pl.when(pl.program_id(2) == 0)
    def _():
        u_ref[...] = (x_ref[0] * (1.0 + sc_ref[0]) + sh_ref[0]).astype(BF16)

    o_ref[0] = _dot(u_ref[...], w_ref[...])


def _in_proj(x, sc, sh, w_bf16, tm):
    b, t, d = x.shape
    n = w_bf16.shape[1]
    tn = 1024
    per_row = sc.shape[1] != 1
    cond_block = (1, tm, d) if per_row else (1, 1, d)
    cond_map = (lambda bi, ti, ni: (bi, ti, 0)) if per_row else (lambda bi, ti, ni: (bi, 0, 0))
    return pl.pallas_call(
        _in_proj_kernel,
        out_shape=jax.ShapeDtypeStruct((b, t, n), F32),
        grid=(b, t // tm, n // tn),
        in_specs=[pl.BlockSpec((1, tm, d), lambda bi, ti, ni: (bi, ti, 0)),
                  pl.BlockSpec(cond_block, cond_map),
                  pl.BlockSpec(cond_block, cond_map),
                  pl.BlockSpec((d, tn), lambda bi, ti, ni: (0, ni))],
        out_specs=pl.BlockSpec((1, tm, tn), lambda bi, ti, ni: (bi, ti, ni)),
        scratch_shapes=[pltpu.VMEM((tm, d), BF16)],
        compiler_params=_cparams(("arbitrary", "arbitrary", "arbitrary")),
        name="in_proj",
    )(x, sc, sh, w_bf16)


def _hgrn_levels(chunk):
    levels = []
    m = HG_SUB
    while m < chunk:
        levels.append(m)
        m *= 2
    return levels


def _hgrn_static(chunk):
    t = jnp.arange(chunk)
    mats = [(t[None, :] <= t[:, None])]
    for m in _hgrn_levels(chunk):
        boundary = (t // (2 * m)) * (2 * m) + m - 1
        mats.append(t[None, :] <= boundary[:, None])
    return jnp.concatenate(mats, axis=0).astype(BF16)


def _hgrn_kernel(p_ref, s0_ref, lb_ref, gn_ref, cum_ref, o_ref, sfin_ref, s_ref, *, chunk, heads, t_valid):
    ci = pl.program_id(1)

    @pl.when(ci == 0)
    def _():
        s_ref[...] = s0_ref[0]

    hw = heads * HEAD_DIM
    row = lax.broadcasted_iota(jnp.int32, (chunk, 1), 0)
    valid = (ci * chunk + row) < t_valid
    rr = lax.broadcasted_iota(jnp.int32, (chunk, chunk), 0)
    cc = lax.broadcasted_iota(jnp.int32, (chunk, chunk), 1)
    nsub = chunk // HG_SUB
    sub_row = lax.broadcasted_iota(jnp.int32, (nsub, HG_SUB, 1), 1)
    levels = _hgrn_levels(chunk)
    cum = cum_ref[...]

    for h in range(heads):
        lo = h * HEAD_DIM
        hq = p_ref[0, :, lo:lo + HEAD_DIM]
        hf = p_ref[0, :, hw + lo:hw + lo + HEAD_DIM]
        v = p_ref[0, :, 2 * hw + lo:2 * hw + lo + HEAD_DIM]
        hg = p_ref[0, :, 3 * hw + lo:3 * hw + lo + HEAD_DIM]
        lb = lb_ref[:, lo:lo + HEAD_DIM]
        f = lb + (1.0 - lb) * jax.nn.sigmoid(hf)
        g = jnp.where(valid, jnp.log(f), 0.0)
        k = jnp.where(valid, 1.0 - f, 0.0)
        q = _silu(hq)

        g_hi = g.astype(BF16)
        g_r = g - g_hi.astype(F32)
        g_mid = g_r.astype(BF16)
        g_lo = (g_r - g_mid.astype(F32)).astype(BF16)
        sums = _dot(cum, g_hi) + _dot(cum, g_mid) + _dot(cum, g_lo)
        b = sums[0:chunk]
        b_last = b[chunk - 1:chunk, :]
        s_old = s_ref[h]

        o = _dot((q * jnp.exp(b)).astype(BF16), s_old.astype(BF16))

        a = jnp.zeros((chunk, chunk), F32)
        for li, m in enumerate(levels):
            bm = sums[(li + 1) * chunk:(li + 2) * chunk]
            upper = (row % (2 * m)) >= m
            qs = jnp.where(upper, q * jnp.exp(jnp.where(upper, b - bm, 0.0)), 0.0)
            ks = jnp.where(upper, 0.0, k * jnp.exp(jnp.where(upper, 0.0, bm - b)))
            same = (rr // (2 * m)) == (cc // (2 * m))
            a = a + jnp.where(same, _dot_nt(qs.astype(BF16), ks.astype(BF16)), 0.0)

        q3 = q.reshape(nsub, HG_SUB, HEAD_DIM)
        k3 = k.reshape(nsub, HG_SUB, HEAD_DIM)
        b3 = b.reshape(nsub, HG_SUB, HEAD_DIM)
        for j in range(HG_SUB):
            kj = k3[:, j:j + 1, :]
            bj = b3[:, j:j + 1, :]
            e = jnp.exp(jnp.where(sub_row >= j, b3 - bj, 0.0))
            col = jnp.sum(q3 * kj * e, axis=-1, keepdims=True).reshape(chunk, 1)
            hit = (cc == (rr // HG_SUB) * HG_SUB + j) & ((rr % HG_SUB) >= j)
            a = a + jnp.where(hit, col, 0.0)

        o = o + _dot(a.astype(BF16), v.astype(BF16))

        kd = k * jnp.exp(b_last - b)
        if chunk < HEAD_DIM:
            pad = jnp.zeros((HEAD_DIM - chunk, HEAD_DIM), F32)
            kd = jnp.concatenate([kd, pad], axis=0)
            vp = jnp.concatenate([v, pad], axis=0)
        else:
            vp = v
        decay_col = jnp.broadcast_to(jnp.exp(b_last), (HEAD_DIM, HEAD_DIM)).T
        s_ref[h] = decay_col * s_old + _dot(kd.T.astype(BF16), vp.astype(BF16))

        ms = jnp.mean(o * o, axis=-1, keepdims=True)
        on = o * lax.rsqrt(ms + RMS_EPS) * gn_ref[...]
        o_ref[0, :, lo:lo + HEAD_DIM] = (on * _silu(hg)).astype(o_ref.dtype)

    @pl.when(ci == pl.num_programs(1) - 1)
    def _():
        sfin_ref[0] = s_ref[...]


def _hgrn(proj, s0, lb, hg_norm, chunk, t_valid):
    b, t, _ = proj.shape
    heads = s0.shape[1]
    hw = heads * HEAD_DIM
    cum = _hgrn_static(chunk)
    kern = functools.partial(_hgrn_kernel, chunk=chunk, heads=heads, t_valid=t_valid)
    return pl.pallas_call(
        kern,
        out_shape=(jax.ShapeDtypeStruct((b, t, hw), BF16),
                   jax.ShapeDtypeStruct(s0.shape, F32)),
        grid=(b, t // chunk),
        in_specs=[pl.BlockSpec((1, chunk, 4 * hw), lambda bi, ci: (bi, ci, 0)),
                  pl.BlockSpec((1, heads, HEAD_DIM, HEAD_DIM), lambda bi, ci: (bi, 0, 0, 0)),
                  pl.BlockSpec((1, hw), lambda bi, ci: (0, 0)),
                  pl.BlockSpec((1, HEAD_DIM), lambda bi, ci: (0, 0)),
                  pl.BlockSpec(cum.shape, lambda bi, ci: (0, 0))],
        out_specs=(pl.BlockSpec((1, chunk, hw), lambda bi, ci: (bi, ci, 0)),
                   pl.BlockSpec((1, heads, HEAD_DIM, HEAD_DIM), lambda bi, ci: (bi, 0, 0, 0))),
        scratch_shapes=[pltpu.VMEM((heads, HEAD_DIM, HEAD_DIM), F32)],
        compiler_params=_cparams(("arbitrary", "arbitrary")),
        name="hgrn",
    )(proj, s0, lb.reshape(1, hw), hg_norm.reshape(1, HEAD_DIM), cum)


def _sb_block(q_bf, k, v, bias, mask, carry, acc, upper_bf):
    z = _dot_nt(q_bf, k.astype(BF16)) + bias
    sp = jnp.maximum(z, 0.0) + jnp.log(1.0 + jnp.exp(-jnp.abs(z)))
    lp = -sp if mask is None else jnp.where(mask, -sp, 0.0)
    lp_hi = lp.astype(BF16)
    lp_lo = (lp - lp_hi.astype(F32)).astype(BF16)
    cs = _dot(lp_hi, upper_bf) + _dot(lp_lo, upper_bf)
    logw = ((z - sp) + (cs - lp)) + carry
    w = jnp.exp(logw)
    if mask is not None:
        w = jnp.where(mask, w, 0.0)
    acc = acc + _dot(w.astype(BF16), v.astype(BF16))
    carry = carry + cs[:, 0:1]
    return carry, acc


def _upper_ones(n):
    r = lax.broadcasted_iota(jnp.int32, (n, n), 0)
    c = lax.broadcasted_iota(jnp.int32, (n, n), 1)
    return (r >= c).astype(BF16)


def _sb_prompt_kernel(bias_ref, q_ref, k_ref, v_ref, gn_ref, o_ref, *, qb, kb, scale, unroll):
    h = pl.program_id(1)
    i = pl.program_id(2)
    bias = bias_ref[h]
    q_bf = (q_ref[0] * scale).astype(BF16)
    upper = _upper_ones(kb)
    q_pos = i * qb + lax.broadcasted_iota(jnp.int32, (qb, kb), 0)
    col = lax.broadcasted_iota(jnp.int32, (qb, kb), 1)
    per_q = qb // kb

    def block(j, masked, state):
        start = pl.multiple_of(j * kb, kb)
        k = k_ref[0, pl.ds(start, kb), :]
        v = v_ref[0, pl.ds(start, kb), :]
        mask = ((start + col) < q_pos) if masked else None
        return _sb_block(q_bf, k, v, bias, mask, state[0], state[1], upper)

    def run(first, count, masked, state):
        for u in range(count):
            state = block(first - u, masked, state)
        return state

    state = (jnp.zeros((qb, 1), F32), jnp.zeros((qb, HEAD_DIM), F32))
    state = run(i * per_q + per_q - 1, per_q, True, state)
    older = i * per_q
    n_groups = older // unroll
    state = lax.fori_loop(0, n_groups, lambda gi, st: run(older - 1 - gi * unroll, unroll, False, st), state)
    left = older - n_groups * unroll
    _, acc = lax.fori_loop(0, left // per_q, lambda s, st: run(left - 1 - s * per_q, per_q, False, st), state)
    ms = jnp.mean(acc * acc, axis=-1, keepdims=True)
    o_ref[0] = (acc * lax.rsqrt(ms + RMS_EPS) * gn_ref[...]).astype(o_ref.dtype)


def _sb_prompt(proj, sb_bias, sb_norm, heads, col0):
    b, t, _ = proj.shape
    qb = 256
    kern = functools.partial(_sb_prompt_kernel, qb=qb, kb=128, scale=1.0 / math.sqrt(HEAD_DIM), unroll=4)
    grid_spec = pltpu.PrefetchScalarGridSpec(
        num_scalar_prefetch=0,
        grid=(b, heads, t // qb),
        in_specs=[pl.BlockSpec(memory_space=pltpu.SMEM),
                  pl.BlockSpec((1, qb, HEAD_DIM), lambda bi, hi, qi: (bi, qi, col0 + hi)),
                  pl.BlockSpec((1, t, HEAD_DIM), lambda bi, hi, qi: (bi, 0, col0 + heads + hi)),
                  pl.BlockSpec((1, t, HEAD_DIM), lambda bi, hi, qi: (bi, 0, col0 + 2 * heads + hi)),
                  pl.BlockSpec((1, HEAD_DIM), lambda bi, hi, qi: (0, 0))],
        out_specs=pl.BlockSpec((1, qb, HEAD_DIM), lambda bi, hi, qi: (bi, qi, hi)),
    )
    return pl.pallas_call(
        kern,
        out_shape=jax.ShapeDtypeStruct((b, t, heads * HEAD_DIM), BF16),
        grid_spec=grid_spec,
        compiler_params=_cparams(("arbitrary", "arbitrary", "arbitrary")),
        name="sb_prompt",
    )(sb_bias, proj, proj, proj, sb_norm.reshape(1, HEAD_DIM))


def _sb_heads_block(q_bf, k_heads, v_heads, bias_col, mask, carry, acc, upper_bf, heads):
    rows = q_bf[0].shape[0]
    z = jnp.concatenate([_dot_nt(q_bf[h], k_heads[h].astype(BF16)) for h in range(heads)], axis=0)
    z = z + bias_col
    sp = jnp.maximum(z, 0.0) + jnp.log(1.0 + jnp.exp(-jnp.abs(z)))
    lp = -sp if mask is None else jnp.where(mask, -sp, 0.0)
    lp_hi = lp.astype(BF16)
    lp_lo = (lp - lp_hi.astype(F32)).astype(BF16)
    cs = _dot(lp_hi, upper_bf) + _dot(lp_lo, upper_bf)
    w = jnp.exp(((z - sp) + (cs - lp)) + carry)
    if mask is not None:
        w = jnp.where(mask, w, 0.0)
    w_bf = w.astype(BF16)
    pv = jnp.concatenate([_dot(w_bf[h * rows:(h + 1) * rows], v_heads[h].astype(BF16))
                          for h in range(heads)], axis=0)
    return carry + cs[:, 0:1], acc + pv


def _sb_sample_kernel(pt_ref, q_ref, ko_ref, vo_ref, kc_hbm, vc_hbm, bias_ref, gn_ref, o_ref,
                      k_buf, v_buf, sem, carry_ref, acc_ref, *, heads, s_valid, scale, pps, n_pages):
    p = pl.program_id(1)
    steps = pl.num_programs(1)
    step = pl.program_id(0) * steps + p
    rows = q_ref.shape[1]
    page = k_buf.shape[3]
    upper = _upper_ones(page)
    bias_col = bias_ref[...]
    q_bf = [(q_ref[0, :, h * HEAD_DIM:(h + 1) * HEAD_DIM] * scale).astype(BF16) for h in range(heads)]

    def fetch(s, slot):
        seq = s // steps
        ps = s - seq * steps
        for j in range(pps):
            pid = pt_ref[seq * n_pages + (n_pages - 1 - (ps * pps + j))]
            for h in range(heads):
                pltpu.make_async_copy(kc_hbm.at[pid, :, h, :], k_buf.at[slot, j, h], sem.at[0, slot]).start()
                pltpu.make_async_copy(vc_hbm.at[pid, :, h, :], v_buf.at[slot, j, h], sem.at[1, slot]).start()

    slot = step % 2

    @pl.when(step == 0)
    def _():
        fetch(step, slot)

    @pl.when(step + 1 < pl.num_programs(0) * steps)
    def _():
        fetch(step + 1, 1 - slot)

    @pl.when(p == 0)
    def _():
        r = lax.broadcasted_iota(jnp.int32, (heads * rows, page), 0) % rows
        c = lax.broadcasted_iota(jnp.int32, (heads * rows, page), 1)
        mask = (c < r) & (c < s_valid)
        pad = jnp.zeros((page - rows, HEAD_DIM), F32)
        k = [jnp.concatenate([ko_ref[0, :, h * HEAD_DIM:(h + 1) * HEAD_DIM], pad], axis=0) for h in range(heads)]
        v = [jnp.concatenate([vo_ref[0, :, h * HEAD_DIM:(h + 1) * HEAD_DIM], pad], axis=0) for h in range(heads)]
        carry, acc = _sb_heads_block(q_bf, k, v, bias_col, mask, jnp.zeros((heads * rows, 1), F32),
                                     jnp.zeros((heads * rows, HEAD_DIM), F32), upper, heads)
        carry_ref[...] = jnp.broadcast_to(carry, carry_ref.shape)
        acc_ref[...] = acc

    pltpu.make_async_copy(k_buf.at[slot], k_buf.at[slot], sem.at[0, slot]).wait()
    pltpu.make_async_copy(v_buf.at[slot], v_buf.at[slot], sem.at[1, slot]).wait()
    carry = carry_ref[:, 0:1]
    acc = acc_ref[...]
    for j in range(pps):
        k = [k_buf[slot, j, h] for h in range(heads)]
        v = [v_buf[slot, j, h] for h in range(heads)]
        carry, acc = _sb_heads_block(q_bf, k, v, bias_col, None, carry, acc, upper, heads)
    carry_ref[...] = jnp.broadcast_to(carry, carry_ref.shape)
    acc_ref[...] = acc

    @pl.when(p == pl.num_programs(1) - 1)
    def _():
        ms = jnp.mean(acc * acc, axis=-1, keepdims=True)
        on = (acc * lax.rsqrt(ms + RMS_EPS) * gn_ref[...]).astype(o_ref.dtype)
        for h in range(heads):
            o_ref[0, :, h * HEAD_DIM:(h + 1) * HEAD_DIM] = on[h * rows:(h + 1) * rows]


def _sb_sample(q, k_own, v_own, cache_k, cache_v, page_ids, sb_bias, sb_norm, heads, s_valid):
    db, rows, hw = q.shape
    n_pages = page_ids.shape[1]
    page = cache_k.shape[1]
    pps = math.gcd(4, n_pages)
    kern = functools.partial(_sb_sample_kernel, heads=heads, s_valid=s_valid, scale=1.0 / math.sqrt(HEAD_DIM),
                             pps=pps, n_pages=n_pages)
    own_map = lambda bi, pi, pt: (bi, 0, 0)
    fixed = lambda bi, pi, pt: (0, 0)
    grid_spec = pltpu.PrefetchScalarGridSpec(
        num_scalar_prefetch=1,
        grid=(db, n_pages // pps),
        in_specs=[pl.BlockSpec((1, rows, hw), own_map),
                  pl.BlockSpec((1, rows, hw), own_map),
                  pl.BlockSpec((1, rows, hw), own_map),
                  pl.BlockSpec(memory_space=pl.ANY),
                  pl.BlockSpec(memory_space=pl.ANY),
                  pl.BlockSpec((heads * rows, 1), fixed),
                  pl.BlockSpec((1, HEAD_DIM), fixed)],
        out_specs=pl.BlockSpec((1, rows, hw), own_map),
        scratch_shapes=[pltpu.VMEM((2, pps, heads, page, HEAD_DIM), F32),
                        pltpu.VMEM((2, pps, heads, page, HEAD_DIM), F32),
                        pltpu.SemaphoreType.DMA((2, 2)),
                        pltpu.VMEM((heads * rows, HEAD_DIM), F32),
                        pltpu.VMEM((heads * rows, HEAD_DIM), F32)],
    )
    bias_col = jnp.repeat(sb_bias.astype(F32), rows).reshape(heads * rows, 1)
    return pl.pallas_call(
        kern,
        out_shape=jax.ShapeDtypeStruct((db, rows, hw), BF16),
        grid_spec=grid_spec,
        compiler_params=_cparams(("arbitrary", "arbitrary")),
        name="sb_sample",
    )(page_ids.reshape(-1), q, k_own, v_own, cache_k, cache_v, bias_col, sb_norm.reshape(1, HEAD_DIM))


def _layer_norm(y, g, b):
    mu = jnp.mean(y, axis=-1, keepdims=True)
    yc = y - mu
    var = jnp.mean(yc * yc, axis=-1, keepdims=True)
    return yc * lax.rsqrt(var + LN_EPS) * g + b


def _out_proj_kernel(oh_ref, os_ref, x_ref, g1_ref, sc2_ref, sh2_ref, wt_ref, wb_ref, lg_ref, lbias_ref,
                     wr_ref, br_ref, x1_ref, u2_ref, route_ref, *, alpha):
    mix = _dot(oh_ref[0], wt_ref[...]) + _dot(os_ref[0], wb_ref[...])
    x1 = _layer_norm(alpha * x_ref[0] + g1_ref[0] * mix, lg_ref[...], lbias_ref[...])
    x1_ref[0] = x1
    u2 = x1 * (1.0 + sc2_ref[0]) + sh2_ref[0]
    u2_ref[0] = u2

    logits = _dot_f32(u2, wr_ref[...]) + br_ref[...]
    lane = lax.broadcasted_iota(jnp.int32, logits.shape, 1)
    neg = -jnp.inf
    gl = jnp.where(lane < N_GROUPS, logits, neg)
    gmax = jnp.max(gl, axis=-1, keepdims=True)
    gidx = jnp.min(jnp.where(gl == gmax, lane, LANE), axis=-1, keepdims=True)
    g_w = 1.0 / jnp.sum(jnp.exp(gl - gmax), axis=-1, keepdims=True)
    in_group = (lane >= N_GROUPS) & (lane < N_GROUPS + N_EXPERTS) & \
               (((lane - N_GROUPS) // EXPERTS_PER_GROUP) == gidx)
    el = jnp.where(in_group, logits, neg)
    v1 = jnp.max(el, axis=-1, keepdims=True)
    i1 = jnp.min(jnp.where(el == v1, lane, LANE), axis=-1, keepdims=True)
    el2 = jnp.where(lane == i1, neg, el)
    v2 = jnp.max(el2, axis=-1, keepdims=True)
    i2 = jnp.min(jnp.where(el2 == v2, lane, LANE), axis=-1, keepdims=True)
    e21 = jnp.exp(v2 - v1)
    p1 = 1.0 / (1.0 + e21)
    p2 = e21 * p1
    route = jnp.where(lane == 0, (i1 - N_GROUPS).astype(F32),
            jnp.where(lane == 1, (i2 - N_GROUPS).astype(F32),
            jnp.where(lane == 2, g_w * p1,
            jnp.where(lane == 3, g_w * p2, 0.0))))
    route_ref[0] = route


def _out_proj(o_h, o_s, x, g1, sc2, sh2, w_o_bf16, ln_g, ln_b, w_r, b_r, alpha, tm):
    b, t, d = x.shape
    hw = o_h.shape[2]
    per_row = g1.shape[1] != 1
    cond_block = (1, tm, d) if per_row else (1, 1, d)
    cond_map = (lambda bi, ti: (bi, ti, 0)) if per_row else (lambda bi, ti: (bi, 0, 0))
    row_map = lambda bi, ti: (bi, ti, 0)
    fixed = lambda bi, ti: (0, 0)
    kern = functools.partial(_out_proj_kernel, alpha=alpha)
    return pl.pallas_call(
        kern,
        out_shape=(jax.ShapeDtypeStruct((b, t, d), F32),
                   jax.ShapeDtypeStruct((b, t, d), F32),
                   jax.ShapeDtypeStruct((b, t, LANE), F32)),
        grid=(b, t // tm),
        in_specs=[pl.BlockSpec((1, tm, hw), row_map),
                  pl.BlockSpec((1, tm, hw), row_map),
                  pl.BlockSpec((1, tm, d), row_map),
                  pl.BlockSpec(cond_block, cond_map),
                  pl.BlockSpec(cond_block, cond_map),
                  pl.BlockSpec(cond_block, cond_map),
                  pl.BlockSpec((hw, d), lambda bi, ti: (0, 0)),
                  pl.BlockSpec((hw, d), lambda bi, ti: (1, 0)),
                  pl.BlockSpec((1, d), fixed),
                  pl.BlockSpec((1, d), fixed),
                  pl.BlockSpec((d, LANE), fixed),
                  pl.BlockSpec((1, LANE), fixed)],
        out_specs=(pl.BlockSpec((1, tm, d), row_map),
                   pl.BlockSpec((1, tm, d), row_map),
                   pl.BlockSpec((1, tm, LANE), row_map)),
        compiler_params=_cparams(("arbitrary", "arbitrary")),
        name="out_proj",
    )(o_h, o_s, x, g1, sc2, sh2, w_o_bf16, w_o_bf16, ln_g.reshape(1, d), ln_b.reshape(1, d), w_r, b_r)


def _row_copy(src_hbm, row, dst, i, sem):
    return pltpu.make_async_copy(src_hbm.at[pl.ds(row, 1)], dst.at[pl.ds(i, 1)], sem)


def _wait_rows(src_hbm, dst, sem):
    pltpu.make_async_copy(src_hbm.at[pl.ds(0, dst.shape[0])], dst, sem).wait()


def _moe_kernel(te_ref, tv_ref, tok_ref, tok_next_ref, u_hbm, wg_ref, wu_ref, wd_ref, y_ref, x_buf, sem, *, tm):
    t = pl.program_id(0)
    slot = t % 2
    valid = tv_ref[t] != 0

    @pl.when(t == 0)
    def _():
        def start(i, c):
            _row_copy(u_hbm, tok_ref[0, 0, i], x_buf.at[0], i, sem.at[0]).start()
            return c

        lax.fori_loop(0, tm, start, 0, unroll=8)

    @pl.when(valid)
    def _():
        _wait_rows(u_hbm, x_buf.at[slot], sem.at[slot])
        for i in range(tm):
            _row_copy(u_hbm, tok_next_ref[0, 0, i], x_buf.at[1 - slot], i, sem.at[1 - slot]).start()
        x = x_buf[slot].astype(BF16)
        hid = _silu(_dot(x, wg_ref[0])) * _dot(x, wu_ref[0])
        y_ref[...] = _dot(hid.astype(BF16), wd_ref[0])

    @pl.when(jnp.logical_not(valid))
    def _():
        y_ref[...] = jnp.zeros_like(y_ref)

        @pl.when(tv_ref[jnp.maximum(t - 1, 0)] != 0)
        def _():
            _wait_rows(u_hbm, x_buf.at[slot], sem.at[slot])


def _moe(u2, tile_expert, tile_valid, slot_token, wg, wu, wd):
    n, d = u2.shape
    tm = MOE_TM
    n_tiles = slot_token.shape[0]
    ff = wg.shape[2]
    kern = functools.partial(_moe_kernel, tm=tm)
    grid_spec = pltpu.PrefetchScalarGridSpec(
        num_scalar_prefetch=2,
        grid=(n_tiles,),
        in_specs=[pl.BlockSpec((1, 1, tm), lambda t, te, tv: (t, 0, 0), memory_space=pltpu.SMEM),
                  pl.BlockSpec((1, 1, tm), lambda t, te, tv: (jnp.minimum(t + 1, n_tiles - 1), 0, 0),
                               memory_space=pltpu.SMEM),
                  pl.BlockSpec(memory_space=pl.ANY),
                  pl.BlockSpec((1, d, ff), lambda t, te, tv: (te[t], 0, 0)),
                  pl.BlockSpec((1, d, ff), lambda t, te, tv: (te[t], 0, 0)),
                  pl.BlockSpec((1, ff, d), lambda t, te, tv: (te[t], 0, 0))],
        out_specs=pl.BlockSpec((tm, d), lambda t, te, tv: (t, 0)),
        scratch_shapes=[pltpu.VMEM((2, tm, d), F32), pltpu.SemaphoreType.DMA((2,))],
    )
    return pl.pallas_call(
        kern,
        out_shape=jax.ShapeDtypeStruct((n_tiles * tm, d), F32),
        grid_spec=grid_spec,
        compiler_params=_cparams(("arbitrary",)),
        name="moe",
    )(tile_expert, tile_valid, slot_token, slot_token, u2, wg, wu, wd)


def _combine_kernel(pos_ref, y_hbm, x1_ref, route_ref, g2_ref, lg_ref, lbias_ref, o_ref, buf0, buf1, sem,
                    *, tm, alpha):
    def start(i, c):
        _row_copy(y_hbm, pos_ref[0, 0, 2 * i], buf0, i, sem.at[0]).start()
        _row_copy(y_hbm, pos_ref[0, 0, 2 * i + 1], buf1, i, sem.at[1]).start()
        return c

    lax.fori_loop(0, tm, start, 0, unroll=8)
    _wait_rows(y_hbm, buf0, sem.at[0])
    _wait_rows(y_hbm, buf1, sem.at[1])
    route = route_ref[0]
    moe = route[:, 2:3] * buf0[...] + route[:, 3:4] * buf1[...]
    o_ref[0] = _layer_norm(alpha * x1_ref[0] + g2_ref[0] * moe, lg_ref[...], lbias_ref[...])


def _combine(y_slots, pos, x1, route, g2, ln_g, ln_b, alpha, tm):
    b, t, d = x1.shape
    nt = t // tm
    per_row = g2.shape[1] != 1
    cond_block = (1, tm, d) if per_row else (1, 1, d)
    cond_map = (lambda bi, ti: (bi, ti, 0)) if per_row else (lambda bi, ti: (bi, 0, 0))
    kern = functools.partial(_combine_kernel, tm=tm, alpha=alpha)
    return pl.pallas_call(
        kern,
        out_shape=jax.ShapeDtypeStruct((b, t, d), F32),
        grid=(b, nt),
        in_specs=[pl.BlockSpec((1, 1, 2 * tm), lambda bi, ti: (bi * nt + ti, 0, 0), memory_space=pltpu.SMEM),
                  pl.BlockSpec(memory_space=pl.ANY),
                  pl.BlockSpec((1, tm, d), lambda bi, ti: (bi, ti, 0)),
                  pl.BlockSpec((1, tm, LANE), lambda bi, ti: (bi, ti, 0)),
                  pl.BlockSpec(cond_block, cond_map),
                  pl.BlockSpec((1, d), lambda bi, ti: (0, 0)),
                  pl.BlockSpec((1, d), lambda bi, ti: (0, 0))],
        out_specs=pl.BlockSpec((1, tm, d), lambda bi, ti: (bi, ti, 0)),
        scratch_shapes=[pltpu.VMEM((tm, d), F32), pltpu.VMEM((tm, d), F32), pltpu.SemaphoreType.DMA((2,))],
        compiler_params=_cparams(("arbitrary", "arbitrary")),
        name="combine",
    )(pos.reshape(b * nt, 1, 2 * tm), y_slots, x1, route, g2, ln_g.reshape(1, d), ln_b.reshape(1, d))


def _dispatch_plan(route, tm):
    n = route.shape[0]
    eid = route[:, 0:2].astype(jnp.int32).reshape(-1)
    onehot = (eid[:, None] == jnp.arange(N_EXPERTS, dtype=jnp.int32)[None, :]).astype(jnp.int32)
    running = jnp.cumsum(onehot, axis=0)
    counts = running[-1]
    padded = ((counts + tm - 1) // tm) * tm
    pad_end = jnp.cumsum(padded)
    pad_off = pad_end - padded
    slot = jnp.sum(onehot * (running - 1 + pad_off[None, :]), axis=1)
    n_tiles = (2 * n) // tm + N_EXPERTS
    n_slots = n_tiles * tm
    tok = jnp.arange(2 * n, dtype=jnp.int32) // 2
    slot_token = jnp.zeros((n_slots,), jnp.int32).at[slot].set(tok, unique_indices=True)
    tile_start = jnp.arange(n_tiles, dtype=jnp.int32) * tm
    tile_expert = jnp.minimum(jnp.sum((tile_start[:, None] >= pad_end[None, :]).astype(jnp.int32), axis=1),
                              N_EXPERTS - 1)
    tile_valid = (tile_start < pad_end[-1]).astype(jnp.int32)
    return tile_expert, tile_valid, slot_token.reshape(n_tiles, 1, tm), slot


def kernel(x_prompt, x_sample, cache_k, cache_v, state_hgrn, page_table, c_prompt, c_sample, hg_lb_logits, w_ada, b_ada, w_in, hg_norm, sb_norm, sb_bias, w_o, ln1_g, ln1_b, w_gr, b_gr, w_er, b_er, w_gate, w_up, w_down, ln2_g, ln2_b):
    depth = w_ada.shape[0]
    assert depth == 1, "single-layer step"
    bp, tp, d = x_prompt.shape
    db, ds, _ = x_sample.shape
    heads = state_hgrn.shape[2]
    hw = heads * HEAD_DIM
    alpha = (2.0 * depth) ** 0.25
    n_phys, page = cache_k.shape[1], cache_k.shape[2]

    def layer0(a):
        return a.reshape(a.shape[1:])

    lower_bounds = jnp.cumsum(jax.nn.softmax(hg_lb_logits.astype(F32), axis=0), axis=0)
    lb = lower_bounds[0]
    hg_norm, sb_norm, sb_bias = layer0(hg_norm), layer0(sb_norm), layer0(sb_bias)
    ln1_g, ln1_b, ln2_g, ln2_b = layer0(ln1_g), layer0(ln1_b), layer0(ln2_g), layer0(ln2_b)

    n_c = bp + db
    c_rows = -(-n_c // SUBLANE) * SUBLANE
    c_all = jnp.concatenate([c_prompt, c_sample, jnp.zeros((c_rows - n_c, d), F32)], axis=0)
    ada = _ada(c_all, layer0(w_ada), layer0(b_ada))
    ada_p = ada[:bp].reshape(bp, 1, 6 * d)
    ada_s = jnp.repeat(ada[bp:bp + db], ds, axis=0).reshape(1, db * ds, 6 * d)

    w_in_bf = layer0(w_in).astype(BF16)
    w_o_bf = layer0(w_o).astype(BF16)
    wg, wu, wd = layer0(w_gate).astype(BF16), layer0(w_up).astype(BF16), layer0(w_down).astype(BF16)
    n_r = N_GROUPS + N_EXPERTS
    w_r = jnp.concatenate([layer0(w_gr), w_er.reshape(d, N_EXPERTS), jnp.zeros((d, LANE - n_r), F32)], axis=1)
    b_r = jnp.concatenate([b_gr.reshape(-1), b_er.reshape(-1), jnp.zeros((LANE - n_r,), F32)]).reshape(1, LANE)

    def split(a):
        return [a[:, :, i * d:(i + 1) * d] for i in range(6)]

    def tail(o_h, o_s, x, g1, sc2, sh2, g2, tm):
        b, t, _ = x.shape
        x1, u2, route = _out_proj(o_h, o_s, x, g1, sc2, sh2, w_o_bf, ln1_g, ln1_b, w_r, b_r, alpha, tm)
        te, tv, slot_token, pos = _dispatch_plan(route.reshape(b * t, LANE), MOE_TM)
        y_slots = _moe(u2.reshape(b * t, d), te, tv, slot_token, wg, wu, wd)
        return _combine(y_slots, pos.reshape(b, t // tm, 1, 2 * tm), x1, route, g2, ln2_g, ln2_b, alpha, tm)

    sh1, sc1, g1, sh2, sc2, g2 = split(ada_p)
    proj_p = _in_proj(x_prompt, sc1, sh1, w_in_bf, 512)
    s0_p = jnp.zeros((bp, heads, HEAD_DIM, HEAD_DIM), F32)
    oh_p, s_p = _hgrn(proj_p, s0_p, lb, hg_norm, HG_CHUNK, tp)
    os_p = _sb_prompt(proj_p, sb_bias, sb_norm, heads, (4 * hw) // HEAD_DIM)
    y_p = tail(oh_p, os_p, x_prompt, g1, sc2, sh2, g2, ROW_TM)
    k_p = proj_p[:, :, 5 * hw:6 * hw].reshape(1, bp, tp, heads, HEAD_DIM)
    v_p = proj_p[:, :, 6 * hw:7 * hw].reshape(1, bp, tp, heads, HEAD_DIM)

    n_s = db * ds
    xs = x_sample.reshape(1, n_s, d)
    sh1, sc1, g1, sh2, sc2, g2 = split(ada_s)
    proj_s = _in_proj(xs, sc1, sh1, w_in_bf, n_s)
    rows = SUBLANE
    proj_s4 = proj_s.reshape(db, ds, -1)
    proj_s8 = jnp.pad(proj_s4, ((0, 0), (0, rows - ds), (0, 0)))
    oh_s8, s_s = _hgrn(proj_s8, layer0(state_hgrn), lb, hg_norm, rows, ds)
    os_s8 = _sb_sample(proj_s8[:, :, 4 * hw:5 * hw], proj_s8[:, :, 5 * hw:6 * hw], proj_s8[:, :, 6 * hw:7 * hw],
                       layer0(cache_k), layer0(cache_v),
                       page_table, sb_bias, sb_norm, heads, ds)
    oh_s = oh_s8[:, :ds].reshape(1, n_s, hw)
    os_s = os_s8[:, :ds].reshape(1, n_s, hw)
    y_s = tail(oh_s, os_s, xs, g1, sc2, sh2, g2, ROW_TM).reshape(db, ds, d)
    k_s = proj_s4[:, :, 5 * hw:6 * hw].reshape(1, db, ds, heads, HEAD_DIM)
    v_s = proj_s4[:, :, 6 * hw:7 * hw].reshape(1, db, ds, heads, HEAD_DIM)

    return (y_p, y_s, k_p, v_p, s_p[None], k_s, v_s, s_s[None])
```

```python
import functools
import math

import jax
import jax.numpy as jnp
from jax import lax
from jax.experimental import pallas as pl
from jax.experimental.pallas import tpu as pltpu

F32 = jnp.float32
BF16 = jnp.bfloat16
HIGHEST = lax.Precision.HIGHEST

LANE = 128
SUBLANE = 8
HEAD_DIM = 128
N_GROUPS = 4
EXPERTS_PER_GROUP = 8
N_EXPERTS = N_GROUPS * EXPERTS_PER_GROUP
LN_EPS = 1e-5
RMS_EPS = 1e-6
VMEM_LIMIT = 56 * 1024 * 1024
HG_CHUNK = 64
HG_SUB = SUBLANE
MOE_TM = 256
ROW_TM = 256


def _cparams(sem):
    return pltpu.CompilerParams(dimension_semantics=sem, vmem_limit_bytes=VMEM_LIMIT)


def _silu(x):
    return x * jax.nn.sigmoid(x)


def _dot(a, b):
    return jnp.dot(a, b, preferred_element_type=F32)


def _dot_nt(a, b):
    return lax.dot_general(a, b, (((1,), (1,)), ((), ())), preferred_element_type=F32)


def _dot_f32(a, b):
    return jnp.dot(a, b, preferred_element_type=F32, precision=HIGHEST)


def _ada_kernel(c_ref, w_ref, b_ref, o_ref):
    o_ref[...] = _dot_f32(_silu(c_ref[...]), w_ref[...]) + b_ref[...]


def _ada(c, w_ada, b_ada):
    rows, d = c.shape
    n = w_ada.shape[1]
    tn = 1024
    return pl.pallas_call(
        _ada_kernel,
        out_shape=jax.ShapeDtypeStruct((rows, n), F32),
        grid=(n // tn,),
        in_specs=[pl.BlockSpec((rows, d), lambda j: (0, 0)),
                  pl.BlockSpec((d, tn), lambda j: (0, j)),
                  pl.BlockSpec((1, tn), lambda j: (0, j))],
        out_specs=pl.BlockSpec((rows, tn), lambda j: (0, j)),
        compiler_params=_cparams(("arbitrary",)),
        name="ada",
    )(c, w_ada, b_ada.reshape(1, n))


def _in_proj_kernel(x_ref, sc_ref, sh_ref, w_ref, o_ref, u_ref):
    @pl.when(pl.program_id(2) == 0)
    def _():
        u_ref[...] = (x_ref[0] * (1.0 + sc_ref[0]) + sh_ref[0]).astype(BF16)

    o_ref[0] = _dot(u_ref[...], w_ref[...])


def _in_proj(x, sc, sh, w_bf16, tm):
    b, t, d = x.shape
    n = w_bf16.shape[1]
    tn = 1024
    per_row = sc.shape[1] != 1
    cond_block = (1, tm, d) if per_row else (1, 1, d)
    cond_map = (lambda bi, ti, ni: (bi, ti, 0)) if per_row else (lambda bi, ti, ni: (bi, 0, 0))
    return pl.pallas_call(
        _in_proj_kernel,
        out_shape=jax.ShapeDtypeStruct((b, t, n), F32),
        grid=(b, t // tm, n // tn),
        in_specs=[pl.BlockSpec((1, tm, d), lambda bi, ti, ni: (bi, ti, 0)),
                  pl.BlockSpec(cond_block, cond_map),
                  pl.BlockSpec(cond_block, cond_map),
                  pl.BlockSpec((d, tn), lambda bi, ti, ni: (0, ni))],
        out_specs=pl.BlockSpec((1, tm, tn), lambda bi, ti, ni: (bi, ti, ni)),
        scratch_shapes=[pltpu.VMEM((tm, d), BF16)],
        compiler_params=_cparams(("arbitrary", "arbitrary", "arbitrary")),
        name="in_proj",
    )(x, sc, sh, w_bf16)


def _hgrn_levels(chunk):
    levels = []
    m = HG_SUB
    while m < chunk:
        levels.append(m)
        m *= 2
    return levels


def _hgrn_static(chunk):
    t = jnp.arange(chunk)
    mats = [(t[None, :] <= t[:, None])]
    for m in _hgrn_levels(chunk):
        boundary = (t // (2 * m)) * (2 * m) + m - 1
        mats.append(t[None, :] <= boundary[:, None])
    return jnp.concatenate(mats, axis=0).astype(BF16)


def _hgrn_kernel(p_ref, s0_ref, lb_ref, gn_ref, cum_ref, o_ref, sfin_ref, s_ref, *, chunk, heads, t_valid):
    ci = pl.program_id(1)

    @pl.when(ci == 0)
    def _():
        s_ref[...] = s0_ref[0]

    hw = heads * HEAD_DIM
    row = lax.broadcasted_iota(jnp.int32, (chunk, 1), 0)
    valid = (ci * chunk + row) < t_valid
    rr = lax.broadcasted_iota(jnp.int32, (chunk, chunk), 0)
    cc = lax.broadcasted_iota(jnp.int32, (chunk, chunk), 1)
    nsub = chunk // HG_SUB
    sub_row = lax.broadcasted_iota(jnp.int32, (nsub, HG_SUB, 1), 1)
    levels = _hgrn_levels(chunk)
    cum = cum_ref[...]

    for h in range(heads):
        lo = h * HEAD_DIM
        hq = p_ref[0, :, lo:lo + HEAD_DIM]
        hf = p_ref[0, :, hw + lo:hw + lo + HEAD_DIM]
        v = p_ref[0, :, 2 * hw + lo:2 * hw + lo + HEAD_DIM]
        hg = p_ref[0, :, 3 * hw + lo:3 * hw + lo + HEAD_DIM]
        lb = lb_ref[:, lo:lo + HEAD_DIM]
        f = lb + (1.0 - lb) * jax.nn.sigmoid(hf)
        g = jnp.where(valid, jnp.log(f), 0.0)
        k = jnp.where(valid, 1.0 - f, 0.0)
        q = _silu(hq)

        g_hi = g.astype(BF16)
        g_r = g - g_hi.astype(F32)
        g_mid = g_r.astype(BF16)
        g_lo = (g_r - g_mid.astype(F32)).astype(BF16)
        sums = _dot(cum, g_hi) + _dot(cum, g_mid) + _dot(cum, g_lo)
        b = sums[0:chunk]
        b_last = b[chunk - 1:chunk, :]
        s_old = s_ref[h]

        o = _dot((q * jnp.exp(b)).astype(BF16), s_old.astype(BF16))

        a = jnp.zeros((chunk, chunk), F32)
        for li, m in enumerate(levels):
            bm = sums[(li + 1) * chunk:(li + 2) * chunk]
            upper = (row % (2 * m)) >= m
            qs = jnp.where(upper, q * jnp.exp(jnp.where(upper, b - bm, 0.0)), 0.0)
            ks = jnp.where(upper, 0.0, k * jnp.exp(jnp.where(upper, 0.0, bm - b)))
            same = (rr // (2 * m)) == (cc // (2 * m))
            a = a + jnp.where(same, _dot_nt(qs.astype(BF16), ks.astype(BF16)), 0.0)

        q3 = q.reshape(nsub, HG_SUB, HEAD_DIM)
        k3 = k.reshape(nsub, HG_SUB, HEAD_DIM)
        b3 = b.reshape(nsub, HG_SUB, HEAD_DIM)
        for j in range(HG_SUB):
            kj = k3[:, j:j + 1, :]
            bj = b3[:, j:j + 1, :]
            e = jnp.exp(jnp.where(sub_row >= j, b3 - bj, 0.0))
            col = jnp.sum(q3 * kj * e, axis=-1, keepdims=True).reshape(chunk, 1)
            hit = (cc == (rr // HG_SUB) * HG_SUB + j) & ((rr % HG_SUB) >= j)
            a = a + jnp.where(hit, col, 0.0)

        o = o + _dot(a.astype(BF16), v.astype(BF16))

        kd = k * jnp.exp(b_last - b)
        if chunk < HEAD_DIM:
            pad = jnp.zeros((HEAD_DIM - chunk, HEAD_DIM), F32)
            kd = jnp.concatenate([kd, pad], axis=0)
            vp = jnp.concatenate([v, pad], axis=0)
        else:
            vp = v
        decay_col = jnp.broadcast_to(jnp.exp(b_last), (HEAD_DIM, HEAD_DIM)).T
        s_ref[h] = decay_col * s_old + _dot(kd.T.astype(BF16), vp.astype(BF16))

        ms = jnp.mean(o * o, axis=-1, keepdims=True)
        on = o * lax.rsqrt(ms + RMS_EPS) * gn_ref[...]
        o_ref[0, :, lo:lo + HEAD_DIM] = (on * _silu(hg)).astype(o_ref.dtype)

    @pl.when(ci == pl.num_programs(1) - 1)
    def _():
        sfin_ref[0] = s_ref[...]


def _hgrn(proj, s0, lb, hg_norm, chunk, t_valid):
    b, t, _ = proj.shape
    heads = s0.shape[1]
    hw = heads * HEAD_DIM
    cum = _hgrn_static(chunk)
    kern = functools.partial(_hgrn_kernel, chunk=chunk, heads=heads, t_valid=t_valid)
    return pl.pallas_call(
        kern,
        out_shape=(jax.ShapeDtypeStruct((b, t, hw), BF16),
                   jax.ShapeDtypeStruct(s0.shape, F32)),
        grid=(b, t // chunk),
        in_specs=[pl.BlockSpec((1, chunk, 4 * hw), lambda bi, ci: (bi, ci, 0)),
                  pl.BlockSpec((1, heads, HEAD_DIM, HEAD_DIM), lambda bi, ci: (bi, 0, 0, 0)),
                  pl.BlockSpec((1, hw), lambda bi, ci: (0, 0)),
                  pl.BlockSpec((1, HEAD_DIM), lambda bi, ci: (0, 0)),
                  pl.BlockSpec(cum.shape, lambda bi, ci: (0, 0))],
        out_specs=(pl.BlockSpec((1, chunk, hw), lambda bi, ci: (bi, ci, 0)),
                   pl.BlockSpec((1, heads, HEAD_DIM, HEAD_DIM), lambda bi, ci: (bi, 0, 0, 0))),
        scratch_shapes=[pltpu.VMEM((heads, HEAD_DIM, HEAD_DIM), F32)],
        compiler_params=_cparams(("arbitrary", "arbitrary")),
        name="hgrn",
    )(proj, s0, lb.reshape(1, hw), hg_norm.reshape(1, HEAD_DIM), cum)


def _sb_block(q_bf, k, v, bias, mask, carry, acc, upper_bf):
    z = _dot_nt(q_bf, k.astype(BF16)) + bias
    sp = jnp.maximum(z, 0.0) + jnp.log(1.0 + jnp.exp(-jnp.abs(z)))
    lp = -sp if mask is None else jnp.where(mask, -sp, 0.0)
    lp_hi = lp.astype(BF16)
    lp_lo = (lp - lp_hi.astype(F32)).astype(BF16)
    cs = _dot(lp_hi, upper_bf) + _dot(lp_lo, upper_bf)
    logw = ((z - sp) + (cs - lp)) + carry
    w = jnp.exp(logw)
    if mask is not None:
        w = jnp.where(mask, w, 0.0)
    acc = acc + _dot(w.astype(BF16), v.astype(BF16))
    carry = carry + cs[:, 0:1]
    return carry, acc


def _upper_ones(n):
    r = lax.broadcasted_iota(jnp.int32, (n, n), 0)
    c = lax.broadcasted_iota(jnp.int32, (n, n), 1)
    return (r >= c).astype(BF16)


def _sb_prompt_kernel(bias_ref, q_ref, k_ref, v_ref, gn_ref, o_ref, *, qb, kb, scale, unroll):
    h = pl.program_id(1)
    i = pl.program_id(2)
    bias = bias_ref[h]
    q_bf = (q_ref[0] * scale).astype(BF16)
    upper = _upper_ones(kb)
    q_pos = i * qb + lax.broadcasted_iota(jnp.int32, (qb, kb), 0)
    col = lax.broadcasted_iota(jnp.int32, (qb, kb), 1)
    per_q = qb // kb

    def block(j, masked, state):
        start = pl.multiple_of(j * kb, kb)
        k = k_ref[0, pl.ds(start, kb), :]
        v = v_ref[0, pl.ds(start, kb), :]
        mask = ((start + col) < q_pos) if masked else None
        return _sb_block(q_bf, k, v, bias, mask, state[0], state[1], upper)

    def run(first, count, masked, state):
        for u in range(count):
            state = block(first - u, masked, state)
        return state

    state = (jnp.zeros((qb, 1), F32), jnp.zeros((qb, HEAD_DIM), F32))
    state = run(i * per_q + per_q - 1, per_q, True, state)
    older = i * per_q
    n_groups = older // unroll
    state = lax.fori_loop(0, n_groups, lambda gi, st: run(older - 1 - gi * unroll, unroll, False, st), state)
    left = older - n_groups * unroll
    _, acc = lax.fori_loop(0, left // per_q, lambda s, st: run(left - 1 - s * per_q, per_q, False, st), state)
    ms = jnp.mean(acc * acc, axis=-1, keepdims=True)
    o_ref[0] = (acc * lax.rsqrt(ms + RMS_EPS) * gn_ref[...]).astype(o_ref.dtype)


def _sb_prompt(proj, sb_bias, sb_norm, heads, col0):
    b, t, _ = proj.shape
    qb = 512
    kern = functools.partial(_sb_prompt_kernel, qb=qb, kb=128, scale=1.0 / math.sqrt(HEAD_DIM), unroll=4)
    grid_spec = pltpu.PrefetchScalarGridSpec(
        num_scalar_prefetch=0,
        grid=(b, heads, t // qb),
        in_specs=[pl.BlockSpec(memory_space=pltpu.SMEM),
                  pl.BlockSpec((1, qb, HEAD_DIM), lambda bi, hi, qi: (bi, qi, col0 + hi)),
                  pl.BlockSpec((1, t, HEAD_DIM), lambda bi, hi, qi: (bi, 0, col0 + heads + hi)),
                  pl.BlockSpec((1, t, HEAD_DIM), lambda bi, hi, qi: (bi, 0, col0 + 2 * heads + hi)),
                  pl.BlockSpec((1, HEAD_DIM), lambda bi, hi, qi: (0, 0))],
        out_specs=pl.BlockSpec((1, qb, HEAD_DIM), lambda bi, hi, qi: (bi, qi, hi)),
    )
    return pl.pallas_call(
        kern,
        out_shape=jax.ShapeDtypeStruct((b, t, heads * HEAD_DIM), BF16),
        grid_spec=grid_spec,
        compiler_params=_cparams(("arbitrary", "arbitrary", "arbitrary")),
        name="sb_prompt",
    )(sb_bias, proj, proj, proj, sb_norm.reshape(1, HEAD_DIM))


def _sb_heads_block(q_bf, k_heads, v_heads, bias_col, mask, carry, acc, upper_bf, heads):
    rows = q_bf[0].shape[0]
    z = jnp.concatenate([_dot_nt(q_bf[h], k_heads[h].astype(BF16)) for h in range(heads)], axis=0)
    z = z + bias_col
    sp = jnp.maximum(z, 0.0) + jnp.log(1.0 + jnp.exp(-jnp.abs(z)))
    lp = -sp if mask is None else jnp.where(mask, -sp, 0.0)
    lp_hi = lp.astype(BF16)
    lp_lo = (lp - lp_hi.astype(F32)).astype(BF16)
    cs = _dot(lp_hi, upper_bf) + _dot(lp_lo, upper_bf)
    w = jnp.exp(((z - sp) + (cs - lp)) + carry)
    if mask is not None:
        w = jnp.where(mask, w, 0.0)
    w_bf = w.astype(BF16)
    pv = jnp.concatenate([_dot(w_bf[h * rows:(h + 1) * rows], v_heads[h].astype(BF16))
                          for h in range(heads)], axis=0)
    return carry + cs[:, 0:1], acc + pv


def _sb_sample_kernel(pt_ref, q_ref, ko_ref, vo_ref, kc_hbm, vc_hbm, bias_ref, gn_ref, o_ref,
                      k_buf, v_buf, sem, carry_ref, acc_ref, *, heads, s_valid, scale, pps, n_pages):
    p = pl.program_id(1)
    steps = pl.num_programs(1)
    step = pl.program_id(0) * steps + p
    rows = q_ref.shape[1]
    page = k_buf.shape[3]
    upper = _upper_ones(page)
    bias_col = bias_ref[...]
    q_bf = [(q_ref[0, :, h * HEAD_DIM:(h + 1) * HEAD_DIM] * scale).astype(BF16) for h in range(heads)]

    def fetch(s, slot):
        seq = s // steps
        ps = s - seq * steps
        for j in range(pps):
            pid = pt_ref[seq * n_pages + (n_pages - 1 - (ps * pps + j))]
            for h in range(heads):
                pltpu.make_async_copy(kc_hbm.at[pid, :, h, :], k_buf.at[slot, j, h], sem.at[0, slot]).start()
                pltpu.make_async_copy(vc_hbm.at[pid, :, h, :], v_buf.at[slot, j, h], sem.at[1, slot]).start()

    slot = step % 2

    @pl.when(step == 0)
    def _():
        fetch(step, slot)

    @pl.when(step + 1 < pl.num_programs(0) * steps)
    def _():
        fetch(step + 1, 1 - slot)

    @pl.when(p == 0)
    def _():
        r = lax.broadcasted_iota(jnp.int32, (heads * rows, page), 0) % rows
        c = lax.broadcasted_iota(jnp.int32, (heads * rows, page), 1)
        mask = (c < r) & (c < s_valid)
        pad = jnp.zeros((page - rows, HEAD_DIM), F32)
        k = [jnp.concatenate([ko_ref[0, :, h * HEAD_DIM:(h + 1) * HEAD_DIM], pad], axis=0) for h in range(heads)]
        v = [jnp.concatenate([vo_ref[0, :, h * HEAD_DIM:(h + 1) * HEAD_DIM], pad], axis=0) for h in range(heads)]
        carry, acc = _sb_heads_block(q_bf, k, v, bias_col, mask, jnp.zeros((heads * rows, 1), F32),
                                     jnp.zeros((heads * rows, HEAD_DIM), F32), upper, heads)
        carry_ref[...] = jnp.broadcast_to(carry, carry_ref.shape)
        acc_ref[...] = acc

    pltpu.make_async_copy(k_buf.at[slot], k_buf.at[slot], sem.at[0, slot]).wait()
    pltpu.make_async_copy(v_buf.at[slot], v_buf.at[slot], sem.at[1, slot]).wait()
    carry = carry_ref[:, 0:1]
    acc = acc_ref[...]
    for j in range(pps):
        k = [k_buf[slot, j, h] for h in range(heads)]
        v = [v_buf[slot, j, h] for h in range(heads)]
        carry, acc = _sb_heads_block(q_bf, k, v, bias_col, None, carry, acc, upper, heads)
    carry_ref[...] = jnp.broadcast_to(carry, carry_ref.shape)
    acc_ref[...] = acc

    @pl.when(p == pl.num_programs(1) - 1)
    def _():
        ms = jnp.mean(acc * acc, axis=-1, keepdims=True)
        on = (acc * lax.rsqrt(ms + RMS_EPS) * gn_ref[...]).astype(o_ref.dtype)
        for h in range(heads):
            o_ref[0, :, h * HEAD_DIM:(h + 1) * HEAD_DIM] = on[h * rows:(h + 1) * rows]


def _sb_sample(q, k_own, v_own, cache_k, cache_v, page_ids, sb_bias, sb_norm, heads, s_valid):
    db, rows, hw = q.shape
    n_pages = page_ids.shape[1]
    page = cache_k.shape[1]
    pps = math.gcd(8, n_pages)
    kern = functools.partial(_sb_sample_kernel, heads=heads, s_valid=s_valid, scale=1.0 / math.sqrt(HEAD_DIM),
                             pps=pps, n_pages=n_pages)
    own_map = lambda bi, pi, pt: (bi, 0, 0)
    fixed = lambda bi, pi, pt: (0, 0)
    grid_spec = pltpu.PrefetchScalarGridSpec(
        num_scalar_prefetch=1,
        grid=(db, n_pages // pps),
        in_specs=[pl.BlockSpec((1, rows, hw), own_map),
                  pl.BlockSpec((1, rows, hw), own_map),
                  pl.BlockSpec((1, rows, hw), own_map),
                  pl.BlockSpec(memory_space=pl.ANY),
                  pl.BlockSpec(memory_space=pl.ANY),
                  pl.BlockSpec((heads * rows, 1), fixed),
                  pl.BlockSpec((1, HEAD_DIM), fixed)],
        out_specs=pl.BlockSpec((1, rows, hw), own_map),
        scratch_shapes=[pltpu.VMEM((2, pps, heads, page, HEAD_DIM), F32),
                        pltpu.VMEM((2, pps, heads, page, HEAD_DIM), F32),
                        pltpu.SemaphoreType.DMA((2, 2)),
                        pltpu.VMEM((heads * rows, HEAD_DIM), F32),
                        pltpu.VMEM((heads * rows, HEAD_DIM), F32)],
    )
    bias_col = jnp.repeat(sb_bias.astype(F32), rows).reshape(heads * rows, 1)
    return pl.pallas_call(
        kern,
        out_shape=jax.ShapeDtypeStruct((db, rows, hw), BF16),
        grid_spec=grid_spec,
        compiler_params=_cparams(("arbitrary", "arbitrary")),
        name="sb_sample",
    )(page_ids.reshape(-1), q, k_own, v_own, cache_k, cache_v, bias_col, sb_norm.reshape(1, HEAD_DIM))


def _layer_norm(y, g, b):
    mu = jnp.mean(y, axis=-1, keepdims=True)
    yc = y - mu
    var = jnp.mean(yc * yc, axis=-1, keepdims=True)
    return yc * lax.rsqrt(var + LN_EPS) * g + b


def _out_proj_kernel(oh_ref, os_ref, x_ref, g1_ref, sc2_ref, sh2_ref, wt_ref, wb_ref, lg_ref, lbias_ref,
                     wr_ref, br_ref, x1_ref, u2_ref, route_ref, *, alpha):
    mix = _dot(oh_ref[0], wt_ref[...]) + _dot(os_ref[0], wb_ref[...])
    x1 = _layer_norm(alpha * x_ref[0] + g1_ref[0] * mix, lg_ref[...], lbias_ref[...])
    x1_ref[0] = x1
    u2 = x1 * (1.0 + sc2_ref[0]) + sh2_ref[0]
    u2_ref[0] = u2

    logits = _dot_f32(u2, wr_ref[...]) + br_ref[...]
    lane = lax.broadcasted_iota(jnp.int32, logits.shape, 1)
    neg = -jnp.inf
    gl = jnp.where(lane < N_GROUPS, logits, neg)
    gmax = jnp.max(gl, axis=-1, keepdims=True)
    gidx = jnp.min(jnp.where(gl == gmax, lane, LANE), axis=-1, keepdims=True)
    g_w = 1.0 / jnp.sum(jnp.exp(gl - gmax), axis=-1, keepdims=True)
    in_group = (lane >= N_GROUPS) & (lane < N_GROUPS + N_EXPERTS) & \
               (((lane - N_GROUPS) // EXPERTS_PER_GROUP) == gidx)
    el = jnp.where(in_group, logits, neg)
    v1 = jnp.max(el, axis=-1, keepdims=True)
    i1 = jnp.min(jnp.where(el == v1, lane, LANE), axis=-1, keepdims=True)
    el2 = jnp.where(lane == i1, neg, el)
    v2 = jnp.max(el2, axis=-1, keepdims=True)
    i2 = jnp.min(jnp.where(el2 == v2, lane, LANE), axis=-1, keepdims=True)
    e21 = jnp.exp(v2 - v1)
    p1 = 1.0 / (1.0 + e21)
    p2 = e21 * p1
    route = jnp.where(lane == 0, (i1 - N_GROUPS).astype(F32),
            jnp.where(lane == 1, (i2 - N_GROUPS).astype(F32),
            jnp.where(lane == 2, g_w * p1,
            jnp.where(lane == 3, g_w * p2, 0.0))))
    route_ref[0] = route


def _out_proj(o_h, o_s, x, g1, sc2, sh2, w_o_bf16, ln_g, ln_b, w_r, b_r, alpha, tm):
    b, t, d = x.shape
    hw = o_h.shape[2]
    per_row = g1.shape[1] != 1
    cond_block = (1, tm, d) if per_row else (1, 1, d)
    cond_map = (lambda bi, ti: (bi, ti, 0)) if per_row else (lambda bi, ti: (bi, 0, 0))
    row_map = lambda bi, ti: (bi, ti, 0)
    fixed = lambda bi, ti: (0, 0)
    kern = functools.partial(_out_proj_kernel, alpha=alpha)
    return pl.pallas_call(
        kern,
        out_shape=(jax.ShapeDtypeStruct((b, t, d), F32),
                   jax.ShapeDtypeStruct((b, t, d), F32),
                   jax.ShapeDtypeStruct((b, t, LANE), F32)),
        grid=(b, t // tm),
        in_specs=[pl.BlockSpec((1, tm, hw), row_map),
                  pl.BlockSpec((1, tm, hw), row_map),
                  pl.BlockSpec((1, tm, d), row_map),
                  pl.BlockSpec(cond_block, cond_map),
                  pl.BlockSpec(cond_block, cond_map),
                  pl.BlockSpec(cond_block, cond_map),
                  pl.BlockSpec((hw, d), lambda bi, ti: (0, 0)),
                  pl.BlockSpec((hw, d), lambda bi, ti: (1, 0)),
                  pl.BlockSpec((1, d), fixed),
                  pl.BlockSpec((1, d), fixed),
                  pl.BlockSpec((d, LANE), fixed),
                  pl.BlockSpec((1, LANE), fixed)],
        out_specs=(pl.BlockSpec((1, tm, d), row_map),
                   pl.BlockSpec((1, tm, d), row_map),
                   pl.BlockSpec((1, tm, LANE), row_map)),
        compiler_params=_cparams(("arbitrary", "arbitrary")),
        name="out_proj",
    )(o_h, o_s, x, g1, sc2, sh2, w_o_bf16, w_o_bf16, ln_g.reshape(1, d), ln_b.reshape(1, d), w_r, b_r)


def _row_copy(src_hbm, row, dst, i, sem):
    return pltpu.make_async_copy(src_hbm.at[pl.ds(row, 1)], dst.at[pl.ds(i, 1)], sem)


def _wait_rows(src_hbm, dst, sem):
    pltpu.make_async_copy(src_hbm.at[pl.ds(0, dst.shape[0])], dst, sem).wait()


def _moe_kernel(te_ref, tv_ref, tok_ref, tok_next_ref, u_hbm, wg_ref, wu_ref, wd_ref, y_ref, x_buf, sem,
                wg_bf, wu_bf, wd_bf, *, tm):
    t = pl.program_id(0)
    slot = t % 2
    valid = tv_ref[t] != 0

    @pl.when(t == 0)
    def _():
        def start(i, c):
            _row_copy(u_hbm, tok_ref[0, 0, i], x_buf.at[0], i, sem.at[0]).start()
            return c

        lax.fori_loop(0, tm, start, 0, unroll=8)

    @pl.when(valid & ((t == 0) | (te_ref[t] != te_ref[jnp.maximum(t - 1, 0)])))
    def _():
        wg_bf[...] = wg_ref[0].astype(BF16)
        wu_bf[...] = wu_ref[0].astype(BF16)
        wd_bf[...] = wd_ref[0].astype(BF16)

    @pl.when(valid)
    def _():
        _wait_rows(u_hbm, x_buf.at[slot], sem.at[slot])
        for i in range(tm):
            _row_copy(u_hbm, tok_next_ref[0, 0, i], x_buf.at[1 - slot], i, sem.at[1 - slot]).start()
        x = x_buf[slot].astype(BF16)
        hid = _silu(_dot(x, wg_bf[...])) * _dot(x, wu_bf[...])
        y_ref[...] = _dot(hid.astype(BF16), wd_bf[...])

    @pl.when(jnp.logical_not(valid))
    def _():
        y_ref[...] = jnp.zeros_like(y_ref)

        @pl.when(tv_ref[jnp.maximum(t - 1, 0)] != 0)
        def _():
            _wait_rows(u_hbm, x_buf.at[slot], sem.at[slot])


def _moe(u2, tile_expert, tile_valid, slot_token, wg, wu, wd):
    n, d = u2.shape
    tm = MOE_TM
    n_tiles = slot_token.shape[0]
    ff = wg.shape[2]
    kern = functools.partial(_moe_kernel, tm=tm)
    grid_spec = pltpu.PrefetchScalarGridSpec(
        num_scalar_prefetch=2,
        grid=(n_tiles,),
        in_specs=[pl.BlockSpec((1, 1, tm), lambda t, te, tv: (t, 0, 0), memory_space=pltpu.SMEM),
                  pl.BlockSpec((1, 1, tm), lambda t, te, tv: (jnp.minimum(t + 1, n_tiles - 1), 0, 0),
                               memory_space=pltpu.SMEM),
                  pl.BlockSpec(memory_space=pl.ANY),
                  pl.BlockSpec((1, d, ff), lambda t, te, tv: (te[t], 0, 0)),
                  pl.BlockSpec((1, d, ff), lambda t, te, tv: (te[t], 0, 0)),
                  pl.BlockSpec((1, ff, d), lambda t, te, tv: (te[t], 0, 0))],
        out_specs=pl.BlockSpec((tm, d), lambda t, te, tv: (t, 0)),
        scratch_shapes=[pltpu.VMEM((2, tm, d), F32), pltpu.SemaphoreType.DMA((2,)),
                        pltpu.VMEM((d, ff), BF16), pltpu.VMEM((d, ff), BF16), pltpu.VMEM((ff, d), BF16)],
    )
    return pl.pallas_call(
        kern,
        out_shape=jax.ShapeDtypeStruct((n_tiles * tm, d), F32),
        grid_spec=grid_spec,
        compiler_params=_cparams(("arbitrary",)),
        name="moe",
    )(tile_expert, tile_valid, slot_token, slot_token, u2, wg, wu, wd)


def _combine_kernel(pos_ref, y_hbm, x1_ref, route_ref, g2_ref, lg_ref, lbias_ref, o_ref, buf0, buf1, sem,
                    *, tm, alpha):
    def start(i, c):
        _row_copy(y_hbm, pos_ref[0, 0, 2 * i], buf0, i, sem.at[0]).start()
        _row_copy(y_hbm, pos_ref[0, 0, 2 * i + 1], buf1, i, sem.at[1]).start()
        return c

    lax.fori_loop(0, tm, start, 0, unroll=8)
    _wait_rows(y_hbm, buf0, sem.at[0])
    _wait_rows(y_hbm, buf1, sem.at[1])
    route = route_ref[0]
    moe = route[:, 2:3] * buf0[...] + route[:, 3:4] * buf1[...]
    o_ref[0] = _layer_norm(alpha * x1_ref[0] + g2_ref[0] * moe, lg_ref[...], lbias_ref[...])


def _combine(y_slots, pos, x1, route, g2, ln_g, ln_b, alpha, tm):
    b, t, d = x1.shape
    nt = t // tm
    per_row = g2.shape[1] != 1
    cond_block = (1, tm, d) if per_row else (1, 1, d)
    cond_map = (lambda bi, ti: (bi, ti, 0)) if per_row else (lambda bi, ti: (bi, 0, 0))
    kern = functools.partial(_combine_kernel, tm=tm, alpha=alpha)
    return pl.pallas_call(
        kern,
        out_shape=jax.ShapeDtypeStruct((b, t, d), F32),
        grid=(b, nt),
        in_specs=[pl.BlockSpec((1, 1, 2 * tm), lambda bi, ti: (bi * nt + ti, 0, 0), memory_space=pltpu.SMEM),
                  pl.BlockSpec(memory_space=pl.ANY),
                  pl.BlockSpec((1, tm, d), lambda bi, ti: (bi, ti, 0)),
                  pl.BlockSpec((1, tm, LANE), lambda bi, ti: (bi, ti, 0)),
                  pl.BlockSpec(cond_block, cond_map),
                  pl.BlockSpec((1, d), lambda bi, ti: (0, 0)),
                  pl.BlockSpec((1, d), lambda bi, ti: (0, 0))],
        out_specs=pl.BlockSpec((1, tm, d), lambda bi, ti: (bi, ti, 0)),
        scratch_shapes=[pltpu.VMEM((tm, d), F32), pltpu.VMEM((tm, d), F32), pltpu.SemaphoreType.DMA((2,))],
        compiler_params=_cparams(("arbitrary", "arbitrary")),
        name="combine",
    )(pos.reshape(b * nt, 1, 2 * tm), y_slots, x1, route, g2, ln_g.reshape(1, d), ln_b.reshape(1, d))


def _dispatch_plan(route, tm):
    n = route.shape[0]
    eid = route[:, 0:2].astype(jnp.int32).reshape(-1)
    onehot = (eid[:, None] == jnp.arange(N_EXPERTS, dtype=jnp.int32)[None, :]).astype(jnp.int32)
    running = jnp.cumsum(onehot, axis=0)
    counts = running[-1]
    padded = ((counts + tm - 1) // tm) * tm
    pad_end = jnp.cumsum(padded)
    pad_off = pad_end - padded
    slot = jnp.sum(onehot * (running - 1 + pad_off[None, :]), axis=1)
    n_tiles = (2 * n) // tm + N_EXPERTS
    n_slots = n_tiles * tm
    tok = jnp.arange(2 * n, dtype=jnp.int32) // 2
    slot_token = jnp.zeros((n_slots,), jnp.int32).at[slot].set(tok, unique_indices=True)
    tile_start = jnp.arange(n_tiles, dtype=jnp.int32) * tm
    tile_expert = jnp.minimum(jnp.sum((tile_start[:, None] >= pad_end[None, :]).astype(jnp.int32), axis=1),
                              N_EXPERTS - 1)
    tile_valid = (tile_start < pad_end[-1]).astype(jnp.int32)
    return tile_expert, tile_valid, slot_token.reshape(n_tiles, 1, tm), slot


def kernel(x_prompt, x_sample, cache_k, cache_v, state_hgrn, page_table, c_prompt, c_sample, hg_lb_logits, w_ada, b_ada, w_in, hg_norm, sb_norm, sb_bias, w_o, ln1_g, ln1_b, w_gr, b_gr, w_er, b_er, w_gate, w_up, w_down, ln2_g, ln2_b):
    depth = w_ada.shape[0]
    assert depth == 1, "single-layer step"
    bp, tp, d = x_prompt.shape
    db, ds, _ = x_sample.shape
    heads = state_hgrn.shape[2]
    hw = heads * HEAD_DIM
    alpha = (2.0 * depth) ** 0.25
    n_phys, page = cache_k.shape[1], cache_k.shape[2]

    def layer0(a):
        return a.reshape(a.shape[1:])

    lower_bounds = jnp.cumsum(jax.nn.softmax(hg_lb_logits.astype(F32), axis=0), axis=0)
    lb = lower_bounds[0]
    hg_norm, sb_norm, sb_bias = layer0(hg_norm), layer0(sb_norm), layer0(sb_bias)
    ln1_g, ln1_b, ln2_g, ln2_b = layer0(ln1_g), layer0(ln1_b), layer0(ln2_g), layer0(ln2_b)

    n_c = bp + db
    c_rows = -(-n_c // SUBLANE) * SUBLANE
    c_all = jnp.concatenate([c_prompt, c_sample, jnp.zeros((c_rows - n_c, d), F32)], axis=0)
    ada = _ada(c_all, layer0(w_ada), layer0(b_ada))
    ada_p = ada[:bp].reshape(bp, 1, 6 * d)
    ada_s = jnp.repeat(ada[bp:bp + db], ds, axis=0).reshape(1, db * ds, 6 * d)

    w_in_bf = layer0(w_in).astype(BF16)
    w_o_bf = layer0(w_o).astype(BF16)
    wg, wu, wd = layer0(w_gate), layer0(w_up), layer0(w_down)
    n_r = N_GROUPS + N_EXPERTS
    w_r = jnp.concatenate([layer0(w_gr), w_er.reshape(d, N_EXPERTS), jnp.zeros((d, LANE - n_r), F32)], axis=1)
    b_r = jnp.concatenate([b_gr.reshape(-1), b_er.reshape(-1), jnp.zeros((LANE - n_r,), F32)]).reshape(1, LANE)

    def split(a):
        return [a[:, :, i * d:(i + 1) * d] for i in range(6)]

    def out_proj(o_h, o_s, x, g1, sc2, sh2):
        return _out_proj(o_h, o_s, x, g1, sc2, sh2, w_o_bf, ln1_g, ln1_b, w_r, b_r, alpha, ROW_TM)

    sh1, sc1, g1, sh2, sc2, g2_p = split(ada_p)
    proj_p = _in_proj(x_prompt, sc1, sh1, w_in_bf, 512)
    s0_p = jnp.zeros((bp, heads, HEAD_DIM, HEAD_DIM), F32)
    oh_p, s_p = _hgrn(proj_p, s0_p, lb, hg_norm, HG_CHUNK, tp)
    os_p = _sb_prompt(proj_p, sb_bias, sb_norm, heads, (4 * hw) // HEAD_DIM)
    x1_p, u2_p, route_p = out_proj(oh_p, os_p, x_prompt, g1, sc2, sh2)
    k_p = proj_p[:, :, 5 * hw:6 * hw].reshape(1, bp, tp, heads, HEAD_DIM)
    v_p = proj_p[:, :, 6 * hw:7 * hw].reshape(1, bp, tp, heads, HEAD_DIM)

    n_s = db * ds
    xs = x_sample.reshape(1, n_s, d)
    sh1, sc1, g1, sh2, sc2, g2_s = split(ada_s)
    proj_s = _in_proj(xs, sc1, sh1, w_in_bf, n_s)
    rows = SUBLANE
    proj_s4 = proj_s.reshape(db, ds, -1)
    proj_s8 = jnp.pad(proj_s4, ((0, 0), (0, rows - ds), (0, 0)))
    oh_s8, s_s = _hgrn(proj_s8, layer0(state_hgrn), lb, hg_norm, rows, ds)
    os_s8 = _sb_sample(proj_s8[:, :, 4 * hw:5 * hw], proj_s8[:, :, 5 * hw:6 * hw], proj_s8[:, :, 6 * hw:7 * hw],
                       layer0(cache_k), layer0(cache_v),
                       page_table, sb_bias, sb_norm, heads, ds)
    oh_s = oh_s8[:, :ds].reshape(1, n_s, hw)
    os_s = os_s8[:, :ds].reshape(1, n_s, hw)
    x1_s, u2_s, route_s = out_proj(oh_s, os_s, xs, g1, sc2, sh2)
    k_s = proj_s4[:, :, 5 * hw:6 * hw].reshape(1, db, ds, heads, HEAD_DIM)
    v_s = proj_s4[:, :, 6 * hw:7 * hw].reshape(1, db, ds, heads, HEAD_DIM)

    n_p = bp * tp
    route_all = jnp.concatenate([route_p.reshape(n_p, LANE), route_s.reshape(n_s, LANE)], axis=0)
    u2_all = jnp.concatenate([u2_p.reshape(n_p, d), u2_s.reshape(n_s, d)], axis=0)
    te, tv, slot_token, pos = _dispatch_plan(route_all, MOE_TM)
    y_slots = _moe(u2_all, te, tv, slot_token, wg, wu, wd)
    tm = ROW_TM
    y_p = _combine(y_slots, pos[:2 * n_p].reshape(bp, tp // tm, 1, 2 * tm), x1_p, route_p, g2_p,
                   ln2_g, ln2_b, alpha, tm)
    y_s = _combine(y_slots, pos[2 * n_p:].reshape(1, n_s // tm, 1, 2 * tm), x1_s, route_s, g2_s,
                   ln2_g, ln2_b, alpha, tm).reshape(db, ds, d)

    return (y_p, y_s, k_p, v_p, s_p[None], k_s, v_s, s_s[None])
```

```python
import functools
import math

import jax
import jax.numpy as jnp
from jax import lax
from jax.experimental import pallas as pl
from jax.experimental.pallas import tpu as pltpu

F32 = jnp.float32
BF16 = jnp.bfloat16
HIGHEST = lax.Precision.HIGHEST

LANE = 128
SUBLANE = 8
HEAD_DIM = 128
N_GROUPS = 4
EXPERTS_PER_GROUP = 8
N_EXPERTS = N_GROUPS * EXPERTS_PER_GROUP
LN_EPS = 1e-5
RMS_EPS = 1e-6
VMEM_LIMIT = 56 * 1024 * 1024
HG_CHUNK = 128
HG_SUB = SUBLANE
MOE_TM = 256
ROW_TM = 256


def _cparams(sem):
    return pltpu.CompilerParams(dimension_semantics=sem, vmem_limit_bytes=VMEM_LIMIT)


def _silu(x):
    return x * jax.nn.sigmoid(x)


def _dot(a, b):
    return jnp.dot(a, b, preferred_element_type=F32)


def _dot_nt(a, b):
    return lax.dot_general(a, b, (((1,), (1,)), ((), ())), preferred_element_type=F32)


def _dot_f32(a, b):
    return jnp.dot(a, b, preferred_element_type=F32, precision=HIGHEST)


def _ada_kernel(c_ref, w_ref, b_ref, o_ref):
    o_ref[...] = _dot_f32(_silu(c_ref[...]), w_ref[...]) + b_ref[...]


def _ada(c, w_ada, b_ada):
    rows, d = c.shape
    n = w_ada.shape[1]
    tn = 1024
    return pl.pallas_call(
        _ada_kernel,
        out_shape=jax.ShapeDtypeStruct((rows, n), F32),
        grid=(n // tn,),
        in_specs=[pl.BlockSpec((rows, d), lambda j: (0, 0)),
                  pl.BlockSpec((d, tn), lambda j: (0, j)),
                  pl.BlockSpec((1, tn), lambda j: (0, j))],
        out_specs=pl.BlockSpec((rows, tn), lambda j: (0, j)),
        compiler_params=_cparams(("arbitrary",)),
        name="ada",
    )(c, w_ada, b_ada.reshape(1, n))


def _in_proj_kernel(x_ref, sc_ref, sh_ref, w_ref, o_ref, u_ref):
    @pl.when(pl.program_id(2) == 0)
    def _():
        u_ref[...] = (x_ref[0] * (1.0 + sc_ref[0]) + sh_ref[0]).astype(BF16)

    o_ref[0] = _dot(u_ref[...], w_ref[...])


def _in_proj(x, sc, sh, w_bf16, tm):
    b, t, d = x.shape
    n = w_bf16.shape[1]
    tn = 1024
    per_row = sc.shape[1] != 1
    cond_block = (1, tm, d) if per_row else (1, 1, d)
    cond_map = (lambda bi, ti, ni: (bi, ti, 0)) if per_row else (lambda bi, ti, ni: (bi, 0, 0))
    return pl.pallas_call(
        _in_proj_kernel,
        out_shape=jax.ShapeDtypeStruct((b, t, n), F32),
        grid=(b, t // tm, n // tn),
        in_specs=[pl.BlockSpec((1, tm, d), lambda bi, ti, ni: (bi, ti, 0)),
                  pl.BlockSpec(cond_block, cond_map),
                  pl.BlockSpec(cond_block, cond_map),
                  pl.BlockSpec((d, tn), lambda bi, ti, ni: (0, ni))],
        out_specs=pl.BlockSpec((1, tm, tn), lambda bi, ti, ni: (bi, ti, ni)),
        scratch_shapes=[pltpu.VMEM((tm, d), BF16)],
        compiler_params=_cparams(("arbitrary", "arbitrary", "arbitrary")),
        name="in_proj",
    )(x, sc, sh, w_bf16)


def _hgrn_levels(chunk):
    levels = []
    m = HG_SUB
    while m < chunk:
        levels.append(m)
        m *= 2
    return levels


def _hgrn_static(chunk):
    t = jnp.arange(chunk)
    mats = [(t[None, :] <= t[:, None])]
    for m in _hgrn_levels(chunk):
        boundary = (t // (2 * m)) * (2 * m) + m - 1
        mats.append(t[None, :] <= boundary[:, None])
    return jnp.concatenate(mats, axis=0).astype(BF16)


def _hgrn_kernel(p_ref, s0_ref, lb_ref, gn_ref, cum_ref, o_ref, sfin_ref, s_ref, *, chunk, heads, t_valid):
    ci = pl.program_id(1)

    @pl.when(ci == 0)
    def _():
        s_ref[...] = s0_ref[0]

    hw = heads * HEAD_DIM
    row = lax.broadcasted_iota(jnp.int32, (chunk, 1), 0)
    valid = (ci * chunk + row) < t_valid
    rr = lax.broadcasted_iota(jnp.int32, (chunk, chunk), 0)
    cc = lax.broadcasted_iota(jnp.int32, (chunk, chunk), 1)
    nsub = chunk // HG_SUB
    sub_row = lax.broadcasted_iota(jnp.int32, (nsub, HG_SUB, 1), 1)
    levels = _hgrn_levels(chunk)
    cum = cum_ref[...]

    for h in range(heads):
        lo = h * HEAD_DIM
        hq = p_ref[0, :, lo:lo + HEAD_DIM]
        hf = p_ref[0, :, hw + lo:hw + lo + HEAD_DIM]
        v = p_ref[0, :, 2 * hw + lo:2 * hw + lo + HEAD_DIM]
        hg = p_ref[0, :, 3 * hw + lo:3 * hw + lo + HEAD_DIM]
        lb = lb_ref[:, lo:lo + HEAD_DIM]
        f = lb + (1.0 - lb) * jax.nn.sigmoid(hf)
        g = jnp.where(valid, jnp.log(f), 0.0)
        k = jnp.where(valid, 1.0 - f, 0.0)
        q = _silu(hq)

        g_hi = g.astype(BF16)
        g_r = g - g_hi.astype(F32)
        g_mid = g_r.astype(BF16)
        g_lo = (g_r - g_mid.astype(F32)).astype(BF16)
        sums = _dot(cum, g_hi) + _dot(cum, g_mid) + _dot(cum, g_lo)
        b = sums[0:chunk]
        b_last = b[chunk - 1:chunk, :]
        s_old = s_ref[h]

        o = _dot((q * jnp.exp(b)).astype(BF16), s_old.astype(BF16))

        a = jnp.zeros((chunk, chunk), F32)
        for li, m in enumerate(levels):
            bm = sums[(li + 1) * chunk:(li + 2) * chunk]
            upper = (row % (2 * m)) >= m
            qs = jnp.where(upper, q * jnp.exp(jnp.where(upper, b - bm, 0.0)), 0.0)
            ks = jnp.where(upper, 0.0, k * jnp.exp(jnp.where(upper, 0.0, bm - b)))
            same = (rr // (2 * m)) == (cc // (2 * m))
            a = a + jnp.where(same, _dot_nt(qs.astype(BF16), ks.astype(BF16)), 0.0)

        q3 = q.reshape(nsub, HG_SUB, HEAD_DIM)
        k3 = k.reshape(nsub, HG_SUB, HEAD_DIM)
        b3 = b.reshape(nsub, HG_SUB, HEAD_DIM)
        for j in range(HG_SUB):
            kj = k3[:, j:j + 1, :]
            bj = b3[:, j:j + 1, :]
            e = jnp.exp(jnp.where(sub_row >= j, b3 - bj, 0.0))
            col = jnp.sum(q3 * kj * e, axis=-1, keepdims=True).reshape(chunk, 1)
            hit = (cc == (rr // HG_SUB) * HG_SUB + j) & ((rr % HG_SUB) >= j)
            a = a + jnp.where(hit, col, 0.0)

        o = o + _dot(a.astype(BF16), v.astype(BF16))

        kd = k * jnp.exp(b_last - b)
        if chunk < HEAD_DIM:
            pad = jnp.zeros((HEAD_DIM - chunk, HEAD_DIM), F32)
            kd = jnp.concatenate([kd, pad], axis=0)
            vp = jnp.concatenate([v, pad], axis=0)
        else:
            vp = v
        decay_col = jnp.broadcast_to(jnp.exp(b_last), (HEAD_DIM, HEAD_DIM)).T
        s_ref[h] = decay_col * s_old + _dot(kd.T.astype(BF16), vp.astype(BF16))

        ms = jnp.mean(o * o, axis=-1, keepdims=True)
        on = o * lax.rsqrt(ms + RMS_EPS) * gn_ref[...]
        o_ref[0, :, lo:lo + HEAD_DIM] = (on * _silu(hg)).astype(o_ref.dtype)

    @pl.when(ci == pl.num_programs(1) - 1)
    def _():
        sfin_ref[0] = s_ref[...]


def _hgrn(proj, s0, lb, hg_norm, chunk, t_valid):
    b, t, _ = proj.shape
    heads = s0.shape[1]
    hw = heads * HEAD_DIM
    cum = _hgrn_static(chunk)
    kern = functools.partial(_hgrn_kernel, chunk=chunk, heads=heads, t_valid=t_valid)
    return pl.pallas_call(
        kern,
        out_shape=(jax.ShapeDtypeStruct((b, t, hw), BF16),
                   jax.ShapeDtypeStruct(s0.shape, F32)),
        grid=(b, t // chunk),
        in_specs=[pl.BlockSpec((1, chunk, 4 * hw), lambda bi, ci: (bi, ci, 0)),
                  pl.BlockSpec((1, heads, HEAD_DIM, HEAD_DIM), lambda bi, ci: (bi, 0, 0, 0)),
                  pl.BlockSpec((1, hw), lambda bi, ci: (0, 0)),
                  pl.BlockSpec((1, HEAD_DIM), lambda bi, ci: (0, 0)),
                  pl.BlockSpec(cum.shape, lambda bi, ci: (0, 0))],
        out_specs=(pl.BlockSpec((1, chunk, hw), lambda bi, ci: (bi, ci, 0)),
                   pl.BlockSpec((1, heads, HEAD_DIM, HEAD_DIM), lambda bi, ci: (bi, 0, 0, 0))),
        scratch_shapes=[pltpu.VMEM((heads, HEAD_DIM, HEAD_DIM), F32)],
        compiler_params=_cparams(("arbitrary", "arbitrary")),
        name="hgrn",
    )(proj, s0, lb.reshape(1, hw), hg_norm.reshape(1, HEAD_DIM), cum)


def _sb_block(q_bf, k, v, bias, mask, carry, acc, upper_bf):
    z = _dot_nt(q_bf, k.astype(BF16)) + bias
    sp = jnp.maximum(z, 0.0) + jnp.log(1.0 + jnp.exp(-jnp.abs(z)))
    lp = -sp if mask is None else jnp.where(mask, -sp, 0.0)
    lp_hi = lp.astype(BF16)
    lp_lo = (lp - lp_hi.astype(F32)).astype(BF16)
    cs = _dot(lp_hi, upper_bf) + _dot(lp_lo, upper_bf)
    logw = ((z - sp) + (cs - lp)) + carry
    w = jnp.exp(logw)
    if mask is not None:
        w = jnp.where(mask, w, 0.0)
    acc = acc + _dot(w.astype(BF16), v.astype(BF16))
    carry = carry + cs[:, 0:1]
    return carry, acc


def _upper_ones(n):
    r = lax.broadcasted_iota(jnp.int32, (n, n), 0)
    c = lax.broadcasted_iota(jnp.int32, (n, n), 1)
    return (r >= c).astype(BF16)


def _sb_prompt_kernel(bias_ref, q_ref, k_ref, v_ref, gn_ref, o_ref, *, qb, kb, scale, unroll):
    h = pl.program_id(1)
    i = pl.program_id(2)
    bias = bias_ref[h]
    q_bf = (q_ref[0] * scale).astype(BF16)
    upper = _upper_ones(kb)
    q_pos = i * qb + lax.broadcasted_iota(jnp.int32, (qb, kb), 0)
    col = lax.broadcasted_iota(jnp.int32, (qb, kb), 1)
    per_q = qb // kb

    def block(j, masked, state):
        start = pl.multiple_of(j * kb, kb)
        k = k_ref[0, pl.ds(start, kb), :]
        v = v_ref[0, pl.ds(start, kb), :]
        mask = ((start + col) < q_pos) if masked else None
        return _sb_block(q_bf, k, v, bias, mask, state[0], state[1], upper)

    def run(first, count, masked, state):
        for u in range(count):
            state = block(first - u, masked, state)
        return state

    state = (jnp.zeros((qb, 1), F32), jnp.zeros((qb, HEAD_DIM), F32))
    state = run(i * per_q + per_q - 1, per_q, True, state)
    older = i * per_q
    n_groups = older // unroll
    state = lax.fori_loop(0, n_groups, lambda gi, st: run(older - 1 - gi * unroll, unroll, False, st), state)
    left = older - n_groups * unroll
    _, acc = lax.fori_loop(0, left // per_q, lambda s, st: run(left - 1 - s * per_q, per_q, False, st), state)
    ms = jnp.mean(acc * acc, axis=-1, keepdims=True)
    o_ref[0] = (acc * lax.rsqrt(ms + RMS_EPS) * gn_ref[...]).astype(o_ref.dtype)


def _sb_prompt(proj, sb_bias, sb_norm, heads, col0):
    b, t, _ = proj.shape
    qb = 512
    kern = functools.partial(_sb_prompt_kernel, qb=qb, kb=128, scale=1.0 / math.sqrt(HEAD_DIM), unroll=4)
    grid_spec = pltpu.PrefetchScalarGridSpec(
        num_scalar_prefetch=0,
        grid=(b, heads, t // qb),
        in_specs=[pl.BlockSpec(memory_space=pltpu.SMEM),
                  pl.BlockSpec((1, qb, HEAD_DIM), lambda bi, hi, qi: (bi, qi, col0 + hi)),
                  pl.BlockSpec((1, t, HEAD_DIM), lambda bi, hi, qi: (bi, 0, col0 + heads + hi)),
                  pl.BlockSpec((1, t, HEAD_DIM), lambda bi, hi, qi: (bi, 0, col0 + 2 * heads + hi)),
                  pl.BlockSpec((1, HEAD_DIM), lambda bi, hi, qi: (0, 0))],
        out_specs=pl.BlockSpec((1, qb, HEAD_DIM), lambda bi, hi, qi: (bi, qi, hi)),
    )
    return pl.pallas_call(
        kern,
        out_shape=jax.ShapeDtypeStruct((b, t, heads * HEAD_DIM), BF16),
        grid_spec=grid_spec,
        compiler_params=_cparams(("arbitrary", "arbitrary", "arbitrary")),
        name="sb_prompt",
    )(sb_bias, proj, proj, proj, sb_norm.reshape(1, HEAD_DIM))


def _sb_heads_block(q_bf, k_heads, v_heads, bias_col, mask, carry, acc, upper_bf, heads):
    rows = q_bf[0].shape[0]
    z = jnp.concatenate([_dot_nt(q_bf[h], k_heads[h].astype(BF16)) for h in range(heads)], axis=0)
    z = z + bias_col
    sp = jnp.maximum(z, 0.0) + jnp.log(1.0 + jnp.exp(-jnp.abs(z)))
    lp = -sp if mask is None else jnp.where(mask, -sp, 0.0)
    lp_hi = lp.astype(BF16)
    lp_lo = (lp - lp_hi.astype(F32)).astype(BF16)
    cs = _dot(lp_hi, upper_bf) + _dot(lp_lo, upper_bf)
    w = jnp.exp(((z - sp) + (cs - lp)) + carry)
    if mask is not None:
        w = jnp.where(mask, w, 0.0)
    w_bf = w.astype(BF16)
    pv = jnp.concatenate([_dot(w_bf[h * rows:(h + 1) * rows], v_heads[h].astype(BF16))
                          for h in range(heads)], axis=0)
    return carry + cs[:, 0:1], acc + pv


def _sb_sample_kernel(pt_ref, q_ref, ko_ref, vo_ref, kc_hbm, vc_hbm, bias_ref, gn_ref, o_ref,
                      k_buf, v_buf, sem, carry_ref, acc_ref, *, heads, s_valid, scale, pps, n_pages):
    p = pl.program_id(1)
    steps = pl.num_programs(1)
    step = pl.program_id(0) * steps + p
    rows = q_ref.shape[1]
    page = k_buf.shape[3]
    upper = _upper_ones(page)
    bias_col = bias_ref[...]
    q_bf = [(q_ref[0, :, h * HEAD_DIM:(h + 1) * HEAD_DIM] * scale).astype(BF16) for h in range(heads)]

    def fetch(s, slot):
        seq = s // steps
        ps = s - seq * steps
        for j in range(pps):
            pid = pt_ref[seq * n_pages + (n_pages - 1 - (ps * pps + j))]
            for h in range(heads):
                pltpu.make_async_copy(kc_hbm.at[pid, :, h, :], k_buf.at[slot, j, h], sem.at[0, slot]).start()
                pltpu.make_async_copy(vc_hbm.at[pid, :, h, :], v_buf.at[slot, j, h], sem.at[1, slot]).start()

    slot = step % 2

    @pl.when(step == 0)
    def _():
        fetch(step, slot)

    @pl.when(step + 1 < pl.num_programs(0) * steps)
    def _():
        fetch(step + 1, 1 - slot)

    @pl.when(p == 0)
    def _():
        r = lax.broadcasted_iota(jnp.int32, (heads * rows, page), 0) % rows
        c = lax.broadcasted_iota(jnp.int32, (heads * rows, page), 1)
        mask = (c < r) & (c < s_valid)
        pad = jnp.zeros((page - rows, HEAD_DIM), F32)
        k = [jnp.concatenate([ko_ref[0, :, h * HEAD_DIM:(h + 1) * HEAD_DIM], pad], axis=0) for h in range(heads)]
        v = [jnp.concatenate([vo_ref[0, :, h * HEAD_DIM:(h + 1) * HEAD_DIM], pad], axis=0) for h in range(heads)]
        carry, acc = _sb_heads_block(q_bf, k, v, bias_col, mask, jnp.zeros((heads * rows, 1), F32),
                                     jnp.zeros((heads * rows, HEAD_DIM), F32), upper, heads)
        carry_ref[...] = jnp.broadcast_to(carry, carry_ref.shape)
        acc_ref[...] = acc

    pltpu.make_async_copy(k_buf.at[slot], k_buf.at[slot], sem.at[0, slot]).wait()
    pltpu.make_async_copy(v_buf.at[slot], v_buf.at[slot], sem.at[1, slot]).wait()
    carry = carry_ref[:, 0:1]
    acc = acc_ref[...]
    for j in range(pps):
        k = [k_buf[slot, j, h] for h in range(heads)]
        v = [v_buf[slot, j, h] for h in range(heads)]
        carry, acc = _sb_heads_block(q_bf, k, v, bias_col, None, carry, acc, upper, heads)
    carry_ref[...] = jnp.broadcast_to(carry, carry_ref.shape)
    acc_ref[...] = acc

    @pl.when(p == pl.num_programs(1) - 1)
    def _():
        ms = jnp.mean(acc * acc, axis=-1, keepdims=True)
        on = (acc * lax.rsqrt(ms + RMS_EPS) * gn_ref[...]).astype(o_ref.dtype)
        for h in range(heads):
            o_ref[0, :, h * HEAD_DIM:(h + 1) * HEAD_DIM] = on[h * rows:(h + 1) * rows]


def _sb_sample(q, k_own, v_own, cache_k, cache_v, page_ids, sb_bias, sb_norm, heads, s_valid):
    db, rows, hw = q.shape
    n_pages = page_ids.shape[1]
    page = cache_k.shape[1]
    pps = math.gcd(8, n_pages)
    kern = functools.partial(_sb_sample_kernel, heads=heads, s_valid=s_valid, scale=1.0 / math.sqrt(HEAD_DIM),
                             pps=pps, n_pages=n_pages)
    own_map = lambda bi, pi, pt: (bi, 0, 0)
    fixed = lambda bi, pi, pt: (0, 0)
    grid_spec = pltpu.PrefetchScalarGridSpec(
        num_scalar_prefetch=1,
        grid=(db, n_pages // pps),
        in_specs=[pl.BlockSpec((1, rows, hw), own_map),
                  pl.BlockSpec((1, rows, hw), own_map),
                  pl.BlockSpec((1, rows, hw), own_map),
                  pl.BlockSpec(memory_space=pl.ANY),
                  pl.BlockSpec(memory_space=pl.ANY),
                  pl.BlockSpec((heads * rows, 1), fixed),
                  pl.BlockSpec((1, HEAD_DIM), fixed)],
        out_specs=pl.BlockSpec((1, rows, hw), own_map),
        scratch_shapes=[pltpu.VMEM((2, pps, heads, page, HEAD_DIM), F32),
                        pltpu.VMEM((2, pps, heads, page, HEAD_DIM), F32),
                        pltpu.SemaphoreType.DMA((2, 2)),
                        pltpu.VMEM((heads * rows, HEAD_DIM), F32),
                        pltpu.VMEM((heads * rows, HEAD_DIM), F32)],
    )
    bias_col = jnp.repeat(sb_bias.astype(F32), rows).reshape(heads * rows, 1)
    return pl.pallas_call(
        kern,
        out_shape=jax.ShapeDtypeStruct((db, rows, hw), BF16),
        grid_spec=grid_spec,
        compiler_params=_cparams(("arbitrary", "arbitrary")),
        name="sb_sample",
    )(page_ids.reshape(-1), q, k_own, v_own, cache_k, cache_v, bias_col, sb_norm.reshape(1, HEAD_DIM))


def _layer_norm(y, g, b):
    mu = jnp.mean(y, axis=-1, keepdims=True)
    yc = y - mu
    var = jnp.mean(yc * yc, axis=-1, keepdims=True)
    return yc * lax.rsqrt(var + LN_EPS) * g + b


def _out_proj_kernel(oh_ref, os_ref, x_ref, g1_ref, sc2_ref, sh2_ref, wt_ref, wb_ref, lg_ref, lbias_ref,
                     wr_ref, br_ref, x1_ref, u2_ref, route_ref, *, alpha):
    mix = _dot(oh_ref[0], wt_ref[...]) + _dot(os_ref[0], wb_ref[...])
    x1 = _layer_norm(alpha * x_ref[0] + g1_ref[0] * mix, lg_ref[...], lbias_ref[...])
    x1_ref[0] = x1
    u2 = x1 * (1.0 + sc2_ref[0]) + sh2_ref[0]
    u2_ref[0] = u2

    logits = _dot_f32(u2, wr_ref[...]) + br_ref[...]
    lane = lax.broadcasted_iota(jnp.int32, logits.shape, 1)
    neg = -jnp.inf
    gl = jnp.where(lane < N_GROUPS, logits, neg)
    gmax = jnp.max(gl, axis=-1, keepdims=True)
    gidx = jnp.min(jnp.where(gl == gmax, lane, LANE), axis=-1, keepdims=True)
    g_w = 1.0 / jnp.sum(jnp.exp(gl - gmax), axis=-1, keepdims=True)
    in_group = (lane >= N_GROUPS) & (lane < N_GROUPS + N_EXPERTS) & \
               (((lane - N_GROUPS) // EXPERTS_PER_GROUP) == gidx)
    el = jnp.where(in_group, logits, neg)
    v1 = jnp.max(el, axis=-1, keepdims=True)
    i1 = jnp.min(jnp.where(el == v1, lane, LANE), axis=-1, keepdims=True)
    el2 = jnp.where(lane == i1, neg, el)
    v2 = jnp.max(el2, axis=-1, keepdims=True)
    i2 = jnp.min(jnp.where(el2 == v2, lane, LANE), axis=-1, keepdims=True)
    e21 = jnp.exp(v2 - v1)
    p1 = 1.0 / (1.0 + e21)
    p2 = e21 * p1
    route = jnp.where(lane == 0, (i1 - N_GROUPS).astype(F32),
            jnp.where(lane == 1, (i2 - N_GROUPS).astype(F32),
            jnp.where(lane == 2, g_w * p1,
            jnp.where(lane == 3, g_w * p2, 0.0))))
    route_ref[0] = route


def _out_proj(o_h, o_s, x, g1, sc2, sh2, w_o_bf16, ln_g, ln_b, w_r, b_r, alpha, tm):
    b, t, d = x.shape
    hw = o_h.shape[2]
    per_row = g1.shape[1] != 1
    cond_block = (1, tm, d) if per_row else (1, 1, d)
    cond_map = (lambda bi, ti: (bi, ti, 0)) if per_row else (lambda bi, ti: (bi, 0, 0))
    row_map = lambda bi, ti: (bi, ti, 0)
    fixed = lambda bi, ti: (0, 0)
    kern = functools.partial(_out_proj_kernel, alpha=alpha)
    return pl.pallas_call(
        kern,
        out_shape=(jax.ShapeDtypeStruct((b, t, d), F32),
                   jax.ShapeDtypeStruct((b, t, d), F32),
                   jax.ShapeDtypeStruct((b, t, LANE), F32)),
        grid=(b, t // tm),
        in_specs=[pl.BlockSpec((1, tm, hw), row_map),
                  pl.BlockSpec((1, tm, hw), row_map),
                  pl.BlockSpec((1, tm, d), row_map),
                  pl.BlockSpec(cond_block, cond_map),
                  pl.BlockSpec(cond_block, cond_map),
                  pl.BlockSpec(cond_block, cond_map),
                  pl.BlockSpec((hw, d), lambda bi, ti: (0, 0)),
                  pl.BlockSpec((hw, d), lambda bi, ti: (1, 0)),
                  pl.BlockSpec((1, d), fixed),
                  pl.BlockSpec((1, d), fixed),
                  pl.BlockSpec((d, LANE), fixed),
                  pl.BlockSpec((1, LANE), fixed)],
        out_specs=(pl.BlockSpec((1, tm, d), row_map),
                   pl.BlockSpec((1, tm, d), row_map),
                   pl.BlockSpec((1, tm, LANE), row_map)),
        compiler_params=_cparams(("arbitrary", "arbitrary")),
        name="out_proj",
    )(o_h, o_s, x, g1, sc2, sh2, w_o_bf16, w_o_bf16, ln_g.reshape(1, d), ln_b.reshape(1, d), w_r, b_r)


def _row_copy(src_hbm, row, dst, i, sem):
    return pltpu.make_async_copy(src_hbm.at[pl.ds(row, 1)], dst.at[pl.ds(i, 1)], sem)


def _wait_rows(src_hbm, dst, sem):
    pltpu.make_async_copy(src_hbm.at[pl.ds(0, dst.shape[0])], dst, sem).wait()


def _moe_kernel(te_ref, tv_ref, tok_ref, tok_next_ref, u_hbm, wg_ref, wu_ref, wd_ref, y_ref, x_buf, sem,
                wg_bf, wu_bf, wd_bf, *, tm):
    t = pl.program_id(0)
    slot = t % 2
    valid = tv_ref[t] != 0

    @pl.when(t == 0)
    def _():
        def start(i, c):
            _row_copy(u_hbm, tok_ref[0, 0, i], x_buf.at[0], i, sem.at[0]).start()
            return c

        lax.fori_loop(0, tm, start, 0, unroll=8)

    @pl.when(valid & ((t == 0) | (te_ref[t] != te_ref[jnp.maximum(t - 1, 0)])))
    def _():
        wg_bf[...] = wg_ref[0].astype(BF16)
        wu_bf[...] = wu_ref[0].astype(BF16)
        wd_bf[...] = wd_ref[0].astype(BF16)

    @pl.when(valid)
    def _():
        _wait_rows(u_hbm, x_buf.at[slot], sem.at[slot])
        for i in range(tm):
            _row_copy(u_hbm, tok_next_ref[0, 0, i], x_buf.at[1 - slot], i, sem.at[1 - slot]).start()
        x = x_buf[slot].astype(BF16)
        hid = _silu(_dot(x, wg_bf[...])) * _dot(x, wu_bf[...])
        y_ref[...] = _dot(hid.astype(BF16), wd_bf[...])

    @pl.when(jnp.logical_not(valid))
    def _():
        y_ref[...] = jnp.zeros_like(y_ref)

        @pl.when(tv_ref[jnp.maximum(t - 1, 0)] != 0)
        def _():
            _wait_rows(u_hbm, x_buf.at[slot], sem.at[slot])


def _moe(u2, tile_expert, tile_valid, slot_token, wg, wu, wd):
    n, d = u2.shape
    tm = MOE_TM
    n_tiles = slot_token.shape[0]
    ff = wg.shape[2]
    kern = functools.partial(_moe_kernel, tm=tm)
    grid_spec = pltpu.PrefetchScalarGridSpec(
        num_scalar_prefetch=2,
        grid=(n_tiles,),
        in_specs=[pl.BlockSpec((1, 1, tm), lambda t, te, tv: (t, 0, 0), memory_space=pltpu.SMEM),
                  pl.BlockSpec((1, 1, tm), lambda t, te, tv: (jnp.minimum(t + 1, n_tiles - 1), 0, 0),
                               memory_space=pltpu.SMEM),
                  pl.BlockSpec(memory_space=pl.ANY),
                  pl.BlockSpec((1, d, ff), lambda t, te, tv: (te[t], 0, 0)),
                  pl.BlockSpec((1, d, ff), lambda t, te, tv: (te[t], 0, 0)),
                  pl.BlockSpec((1, ff, d), lambda t, te, tv: (te[t], 0, 0))],
        out_specs=pl.BlockSpec((tm, d), lambda t, te, tv: (t, 0)),
        scratch_shapes=[pltpu.VMEM((2, tm, d), F32), pltpu.SemaphoreType.DMA((2,)),
                        pltpu.VMEM((d, ff), BF16), pltpu.VMEM((d, ff), BF16), pltpu.VMEM((ff, d), BF16)],
    )
    return pl.pallas_call(
        kern,
        out_shape=jax.ShapeDtypeStruct((n_tiles * tm, d), F32),
        grid_spec=grid_spec,
        compiler_params=_cparams(("arbitrary",)),
        name="moe",
    )(tile_expert, tile_valid, slot_token, slot_token, u2, wg, wu, wd)


def _combine_kernel(pos_ref, y_hbm, x1_ref, route_ref, g2_ref, lg_ref, lbias_ref, o_ref, buf0, buf1, sem,
                    *, tm, alpha):
    def start(i, c):
        _row_copy(y_hbm, pos_ref[0, 0, 2 * i], buf0, i, sem.at[0]).start()
        _row_copy(y_hbm, pos_ref[0, 0, 2 * i + 1], buf1, i, sem.at[1]).start()
        return c

    lax.fori_loop(0, tm, start, 0, unroll=8)
    _wait_rows(y_hbm, buf0, sem.at[0])
    _wait_rows(y_hbm, buf1, sem.at[1])
    route = route_ref[0]
    moe = route[:, 2:3] * buf0[...] + route[:, 3:4] * buf1[...]
    o_ref[0] = _layer_norm(alpha * x1_ref[0] + g2_ref[0] * moe, lg_ref[...], lbias_ref[...])


def _combine(y_slots, pos, x1, route, g2, ln_g, ln_b, alpha, tm):
    b, t, d = x1.shape
    nt = t // tm
    per_row = g2.shape[1] != 1
    cond_block = (1, tm, d) if per_row else (1, 1, d)
    cond_map = (lambda bi, ti: (bi, ti, 0)) if per_row else (lambda bi, ti: (bi, 0, 0))
    kern = functools.partial(_combine_kernel, tm=tm, alpha=alpha)
    return pl.pallas_call(
        kern,
        out_shape=jax.ShapeDtypeStruct((b, t, d), F32),
        grid=(b, nt),
        in_specs=[pl.BlockSpec((1, 1, 2 * tm), lambda bi, ti: (bi * nt + ti, 0, 0), memory_space=pltpu.SMEM),
                  pl.BlockSpec(memory_space=pl.ANY),
                  pl.BlockSpec((1, tm, d), lambda bi, ti: (bi, ti, 0)),
                  pl.BlockSpec((1, tm, LANE), lambda bi, ti: (bi, ti, 0)),
                  pl.BlockSpec(cond_block, cond_map),
                  pl.BlockSpec((1, d), lambda bi, ti: (0, 0)),
                  pl.BlockSpec((1, d), lambda bi, ti: (0, 0))],
        out_specs=pl.BlockSpec((1, tm, d), lambda bi, ti: (bi, ti, 0)),
        scratch_shapes=[pltpu.VMEM((tm, d), F32), pltpu.VMEM((tm, d), F32), pltpu.SemaphoreType.DMA((2,))],
        compiler_params=_cparams(("arbitrary", "arbitrary")),
        name="combine",
    )(pos.reshape(b * nt, 1, 2 * tm), y_slots, x1, route, g2, ln_g.reshape(1, d), ln_b.reshape(1, d))


def _dispatch_plan(route, tm):
    n = route.shape[0]
    eid = route[:, 0:2].astype(jnp.int32).reshape(-1)
    onehot = (eid[:, None] == jnp.arange(N_EXPERTS, dtype=jnp.int32)[None, :]).astype(jnp.int32)
    running = jnp.cumsum(onehot, axis=0)
    counts = running[-1]
    padded = ((counts + tm - 1) // tm) * tm
    pad_end = jnp.cumsum(padded)
    pad_off = pad_end - padded
    slot = jnp.sum(onehot * (running - 1 + pad_off[None, :]), axis=1)
    n_tiles = (2 * n) // tm + N_EXPERTS
    n_slots = n_tiles * tm
    tok = jnp.arange(2 * n, dtype=jnp.int32) // 2
    slot_token = jnp.zeros((n_slots,), jnp.int32).at[slot].set(tok, unique_indices=True)
    tile_start = jnp.arange(n_tiles, dtype=jnp.int32) * tm
    tile_expert = jnp.minimum(jnp.sum((tile_start[:, None] >= pad_end[None, :]).astype(jnp.int32), axis=1),
                              N_EXPERTS - 1)
    tile_valid = (tile_start < pad_end[-1]).astype(jnp.int32)
    return tile_expert, tile_valid, slot_token.reshape(n_tiles, 1, tm), slot


def kernel(x_prompt, x_sample, cache_k, cache_v, state_hgrn, page_table, c_prompt, c_sample, hg_lb_logits, w_ada, b_ada, w_in, hg_norm, sb_norm, sb_bias, w_o, ln1_g, ln1_b, w_gr, b_gr, w_er, b_er, w_gate, w_up, w_down, ln2_g, ln2_b):
    depth = w_ada.shape[0]
    assert depth == 1, "single-layer step"
    bp, tp, d = x_prompt.shape
    db, ds, _ = x_sample.shape
    heads = state_hgrn.shape[2]
    hw = heads * HEAD_DIM
    alpha = (2.0 * depth) ** 0.25
    n_phys, page = cache_k.shape[1], cache_k.shape[2]

    def layer0(a):
        return a.reshape(a.shape[1:])

    lower_bounds = jnp.cumsum(jax.nn.softmax(hg_lb_logits.astype(F32), axis=0), axis=0)
    lb = lower_bounds[0]
    hg_norm, sb_norm, sb_bias = layer0(hg_norm), layer0(sb_norm), layer0(sb_bias)
    ln1_g, ln1_b, ln2_g, ln2_b = layer0(ln1_g), layer0(ln1_b), layer0(ln2_g), layer0(ln2_b)

    n_c = bp + db
    c_rows = -(-n_c // SUBLANE) * SUBLANE
    c_all = jnp.concatenate([c_prompt, c_sample, jnp.zeros((c_rows - n_c, d), F32)], axis=0)
    ada = _ada(c_all, layer0(w_ada), layer0(b_ada))
    ada_p = ada[:bp].reshape(bp, 1, 6 * d)
    ada_s = jnp.repeat(ada[bp:bp + db], ds, axis=0).reshape(1, db * ds, 6 * d)

    w_in_bf = layer0(w_in).astype(BF16)
    w_o_bf = layer0(w_o).astype(BF16)
    wg, wu, wd = layer0(w_gate), layer0(w_up), layer0(w_down)
    n_r = N_GROUPS + N_EXPERTS
    w_r = jnp.concatenate([layer0(w_gr), w_er.reshape(d, N_EXPERTS), jnp.zeros((d, LANE - n_r), F32)], axis=1)
    b_r = jnp.concatenate([b_gr.reshape(-1), b_er.reshape(-1), jnp.zeros((LANE - n_r,), F32)]).reshape(1, LANE)

    def split(a):
        return [a[:, :, i * d:(i + 1) * d] for i in range(6)]

    def out_proj(o_h, o_s, x, g1, sc2, sh2):
        return _out_proj(o_h, o_s, x, g1, sc2, sh2, w_o_bf, ln1_g, ln1_b, w_r, b_r, alpha, ROW_TM)

    sh1, sc1, g1, sh2, sc2, g2_p = split(ada_p)
    proj_p = _in_proj(x_prompt, sc1, sh1, w_in_bf, 512)
    s0_p = jnp.zeros((bp, heads, HEAD_DIM, HEAD_DIM), F32)
    oh_p, s_p = _hgrn(proj_p, s0_p, lb, hg_norm, HG_CHUNK, tp)
    os_p = _sb_prompt(proj_p, sb_bias, sb_norm, heads, (4 * hw) // HEAD_DIM)
    x1_p, u2_p, route_p = out_proj(oh_p, os_p, x_prompt, g1, sc2, sh2)
    k_p = proj_p[:, :, 5 * hw:6 * hw].reshape(1, bp, tp, heads, HEAD_DIM)
    v_p = proj_p[:, :, 6 * hw:7 * hw].reshape(1, bp, tp, heads, HEAD_DIM)

    n_s = db * ds
    xs = x_sample.reshape(1, n_s, d)
    sh1, sc1, g1, sh2, sc2, g2_s = split(ada_s)
    proj_s = _in_proj(xs, sc1, sh1, w_in_bf, n_s)
    rows = SUBLANE
    proj_s4 = proj_s.reshape(db, ds, -1)
    proj_s8 = jnp.pad(proj_s4, ((0, 0), (0, rows - ds), (0, 0)))
    oh_s8, s_s = _hgrn(proj_s8, layer0(state_hgrn), lb, hg_norm, rows, ds)
    os_s8 = _sb_sample(proj_s8[:, :, 4 * hw:5 * hw], proj_s8[:, :, 5 * hw:6 * hw], proj_s8[:, :, 6 * hw:7 * hw],
                       layer0(cache_k), layer0(cache_v),
                       page_table, sb_bias, sb_norm, heads, ds)
    oh_s = oh_s8[:, :ds].reshape(1, n_s, hw)
    os_s = os_s8[:, :ds].reshape(1, n_s, hw)
    x1_s, u2_s, route_s = out_proj(oh_s, os_s, xs, g1, sc2, sh2)
    k_s = proj_s4[:, :, 5 * hw:6 * hw].reshape(1, db, ds, heads, HEAD_DIM)
    v_s = proj_s4[:, :, 6 * hw:7 * hw].reshape(1, db, ds, heads, HEAD_DIM)

    n_p = bp * tp
    route_all = jnp.concatenate([route_p.reshape(n_p, LANE), route_s.reshape(n_s, LANE)], axis=0)
    u2_all = jnp.concatenate([u2_p.reshape(n_p, d), u2_s.reshape(n_s, d)], axis=0)
    te, tv, slot_token, pos = _dispatch_plan(route_all, MOE_TM)
    y_slots = _moe(u2_all, te, tv, slot_token, wg, wu, wd)
    tm = ROW_TM
    y_p = _combine(y_slots, pos[:2 * n_p].reshape(bp, tp // tm, 1, 2 * tm), x1_p, route_p, g2_p,
                   ln2_g, ln2_b, alpha, tm)
    y_s = _combine(y_slots, pos[2 * n_p:].reshape(1, n_s // tm, 1, 2 * tm), x1_s, route_s, g2_s,
                   ln2_g, ln2_b, alpha, tm).reshape(db, ds, d)

    return (y_p, y_s, k_p, v_p, s_p[None], k_s, v_s, s_s[None])
```
